```python
import jax, jax.numpy as jnp
from jax import lax
import numpy as np

D_MODEL = 1024
BATCH = 16
SEQ = 2048
DEPTH = 1

A_HEADS = 4
A_HEAD_DIM = 128
A_WIDTH = A_HEADS * A_HEAD_DIM
A_CONV = 4
A_CHUNK = 64
B_HEADS = 8
B_HEAD_DIM = 64
B_WIDTH = B_HEADS * B_HEAD_DIM
B_DECAY_LORA = 64
B_ICLR_LORA = 64
B_GATE_LORA = 128
B_GN_EPS = 64e-5
N_EXPERTS = 32
TOP_K = 4
D_EXPERT = 1024
SWIGLU_ALPHA = 1.702
SWIGLU_LIMIT = 7.0
RMS_EPS = 1e-5
L2_EPS = 1e-6

A_COLS = 4 * A_WIDTH + 2 * A_HEADS
B_COLS = 3 * B_WIDTH + B_DECAY_LORA + B_ICLR_LORA + B_GATE_LORA
G_COLS = 2 * D_MODEL
IN_COLS = A_COLS + B_COLS + G_COLS

kernel_name = "hybrid_gdn_rwkv7_moe_block"


def rmsnorm(x, g, eps=RMS_EPS):
    xf = x.astype(jnp.float32)
    y = xf * lax.rsqrt(jnp.mean(xf * xf, axis=-1, keepdims=True) + eps)
    return (y * g.astype(jnp.float32)).astype(x.dtype)


def l2norm(x, eps=L2_EPS):
    return x * lax.rsqrt(jnp.sum(x * x, axis=-1, keepdims=True) + eps)


def causal_depthwise_conv(x, w):
    k_w, c = w.shape
    return lax.conv_general_dilated(x, w[:, None, :], window_strides=(1,), padding=[(k_w - 1, 0)],
                                    dimension_numbers=('NWC', 'WIO', 'NWC'), feature_group_count=c)


def token_shift(x):
    return jnp.pad(x, ((0, 0), (1, 0), (0, 0)))[:, :-1]


def gated_delta_rule_chunked(q, k, v, g, beta):
    bn, t_len, nh, dk = q.shape
    dv = v.shape[-1]
    c = A_CHUNK
    n = t_len // c

    def to_chunks(t):
        t = t.reshape((bn, n, c, nh) + t.shape[3:])
        return jnp.moveaxis(t, 3, 1)

    q, k, v = to_chunks(q), to_chunks(k), to_chunks(v)
    g, beta = to_chunks(g), to_chunks(beta)
    gc = jnp.cumsum(g, axis=-1)
    causal = jnp.tril(jnp.ones((c, c), dtype=bool))
    strict = jnp.tril(jnp.ones((c, c), dtype=bool), -1)
    decay = jnp.exp(jnp.where(causal, gc[..., :, None] - gc[..., None, :], -jnp.inf))
    kb = k * beta[..., None]
    lmat = jnp.where(strict, jnp.einsum('bhncd,bhnsd->bhncs', kb, k) * decay, 0.0)
    eye = jnp.eye(c, dtype=q.dtype)
    tmat = lax.linalg.triangular_solve(eye + lmat, jnp.broadcast_to(eye, lmat.shape),
                                       left_side=True, lower=True)
    u = jnp.einsum('bhncs,bhnse->bhnce', tmat, v * beta[..., None])
    w = jnp.einsum('bhncs,bhnsd->bhncd', tmat, kb * jnp.exp(gc)[..., None])
    attn = jnp.einsum('bhncd,bhnsd->bhncs', q, k) * decay
    qg = q * jnp.exp(gc)[..., None]
    kg = k * jnp.exp(gc[..., -1:] - gc)[..., None]
    glast = jnp.exp(gc[..., -1])
    xs = [jnp.moveaxis(t, 2, 0) for t in (u, w, attn, qg, kg, glast)]

    def step(s, inp):
        u_c, w_c, a_c, qg_c, kg_c, gl_c = inp
        v_new = u_c - jnp.einsum('bhcd,bhde->bhce', w_c, s)
        o_c = jnp.einsum('bhcd,bhde->bhce', qg_c, s) + jnp.einsum('bhcs,bhse->bhce', a_c, v_new)
        s = s * gl_c[..., None, None] + jnp.einsum('bhcd,bhce->bhde', kg_c, v_new)
        return s, o_c

    s0 = jnp.zeros((bn, nh, dk, dv), q.dtype)
    _, o = lax.scan(step, s0, xs)
    o = jnp.moveaxis(o, 0, 2)
    return jnp.moveaxis(o, 1, 3).reshape(bn, t_len, nh, dv)


def rwkv7_recurrence(r, w, k, v, kk, a):
    bn, _, nh, nd = r.shape

    def step(s, inp):
        r_t, w_t, k_t, v_t, kk_t, a_t = inp
        sa = jnp.einsum('bhij,bhj->bhi', s, -kk_t)
        s = (s * w_t[:, :, None, :] + sa[..., None] * (kk_t * a_t)[:, :, None, :]
             + v_t[..., None] * k_t[:, :, None, :])
        return s, jnp.einsum('bhij,bhj->bhi', s, r_t)

    xs = [jnp.moveaxis(t, 1, 0) for t in (r, w, k, v, kk, a)]
    s0 = jnp.zeros((bn, nh, nd, nd), r.dtype)
    _, out = lax.scan(step, s0, xs)
    return jnp.moveaxis(out, 0, 1)


def mixer_a(pa, conv_w, a_log, dt_bias, norm_g):
    bn, t_len, _ = pa.shape
    f32 = jnp.float32
    qkv = jax.nn.silu(causal_depthwise_conv(pa[..., :3 * A_WIDTH], conv_w))
    q, k, v = jnp.split(qkv, 3, axis=-1)
    heads = lambda t: t.reshape(bn, t_len, A_HEADS, A_HEAD_DIM).astype(f32)
    q = l2norm(heads(q)) * (A_HEAD_DIM ** -0.5)
    k = l2norm(heads(k))
    v = heads(v)
    z = pa[..., 3 * A_WIDTH:4 * A_WIDTH].astype(f32)
    beta = jax.nn.sigmoid(pa[..., 4 * A_WIDTH:4 * A_WIDTH + A_HEADS].astype(f32))
    g = -jnp.exp(a_log.astype(f32)) * jax.nn.softplus(pa[..., 4 * A_WIDTH + A_HEADS:].astype(f32)
                                                      + dt_bias.astype(f32))
    o = gated_delta_rule_chunked(q, k, v, g, beta)
    o = o * lax.rsqrt(jnp.mean(o * o, axis=-1, keepdims=True) + L2_EPS) * norm_g.astype(f32)
    o = o.reshape(bn, t_len, A_WIDTH) * jax.nn.silu(z)
    return o.astype(pa.dtype)


def mixer_b(pb, mu, w0, w2, a0, a2, g2, k_k, k_a, r_k, ln_w, ln_b):
    bn, t_len, _ = pb.shape
    f32 = jnp.float32
    xm = pb + (token_shift(pb) - pb) * mu
    r, k, v, xw, xa, xg = jnp.split(xm, np.cumsum([B_WIDTH, B_WIDTH, B_WIDTH, B_DECAY_LORA, B_ICLR_LORA]).tolist(), axis=-1)
    wl = -jax.nn.softplus(-(w0 + jnp.tanh(xw) @ w2).astype(f32)) - 0.5
    decay = jnp.exp(-jnp.exp(wl))
    a = jax.nn.sigmoid((a0 + xa @ a2).astype(f32))
    gate = (jax.nn.sigmoid(xg) @ g2).astype(f32)
    heads = lambda t: t.reshape(bn, t_len, B_HEADS, B_HEAD_DIM)
    r = heads(r.astype(f32))
    v = heads(v.astype(f32))
    kf = k.astype(f32)
    kk = l2norm(heads(kf * k_k.astype(f32)))
    kf = heads(kf * (1.0 + (a - 1.0) * k_a.astype(f32)))
    a_h = heads(a)
    out = rwkv7_recurrence(r, heads(decay), kf, v, kk, a_h)
    mean = jnp.mean(out, axis=-1, keepdims=True)
    var = jnp.mean(jnp.square(out - mean), axis=-1, keepdims=True)
    out = (out - mean) * lax.rsqrt(var + B_GN_EPS)
    out = out.reshape(bn, t_len, B_WIDTH) * ln_w.astype(f32) + ln_b.astype(f32)
    bonus = jnp.sum(r * kf * r_k.astype(f32), axis=-1, keepdims=True) * v
    out = (out + bonus.reshape(bn, t_len, B_WIDTH)) * gate
    return out.astype(pb.dtype)


def moe_ffn(h, w_router, b_router, w1, b1, w2, b2):
    bn, t_len, d = h.shape
    ht = h.reshape(-1, d)
    logits = (ht @ w_router).astype(jnp.float32) + b_router.astype(jnp.float32)
    top_logits, top_idx = lax.top_k(logits, TOP_K)
    gates = jax.nn.softmax(top_logits, axis=-1)
    flat_e = top_idx.reshape(-1)
    order = jnp.argsort(flat_e)
    sorted_e = flat_e[order]
    tok = order // TOP_K
    group_sizes = jnp.bincount(flat_e, length=N_EXPERTS).astype(jnp.int32)
    xs = ht[tok]
    h1 = lax.ragged_dot(xs, w1, group_sizes) + b1[sorted_e]
    glu, lin = h1[:, :D_EXPERT], h1[:, D_EXPERT:]
    glu = jnp.minimum(glu, SWIGLU_LIMIT)
    lin = jnp.clip(lin, -SWIGLU_LIMIT, SWIGLU_LIMIT)
    act = glu * jax.nn.sigmoid(SWIGLU_ALPHA * glu) * (lin + 1.0)
    y = lax.ragged_dot(act, w2, group_sizes) + b2[sorted_e]
    y = y * gates.reshape(-1)[order][:, None].astype(y.dtype)
    out = jax.ops.segment_sum(y, tok, num_segments=ht.shape[0])
    return out.reshape(bn, t_len, d)


def setup_inputs(seed: int = 0) -> dict:
    key = jax.random.key(seed)
    ks = jax.random.split(key, 32)
    nrm = lambda k, s, sc: jax.random.normal(k, s, jnp.float32) * sc
    L = DEPTH
    dt = jnp.exp(jax.random.uniform(ks[5], (L, A_HEADS), jnp.float32, np.log(1e-3), np.log(0.1)))
    return {
        "x": nrm(ks[0], (BATCH, SEQ, D_MODEL), 1.0),
        "norm1_g": 1.0 + nrm(ks[1], (L, D_MODEL), 0.01),
        "w_in": nrm(ks[2], (L, D_MODEL, IN_COLS), D_MODEL ** -0.5),
        "a_conv_w": nrm(ks[3], (L, A_CONV, 3 * A_WIDTH), A_CONV ** -0.5),
        "a_A_log": jnp.log(jax.random.uniform(ks[4], (L, A_HEADS), jnp.float32, 1.0, 16.0)),
        "a_dt_bias": dt + jnp.log(-jnp.expm1(-dt)),
        "a_norm_g": 1.0 + nrm(ks[6], (L, A_HEAD_DIM), 0.01),
        "b_mu": jax.random.uniform(ks[7], (L, B_COLS), jnp.float32),
        "b_w0": jax.random.uniform(ks[8], (L, B_WIDTH), jnp.float32, -5.0, 0.0),
        "b_w2": nrm(ks[9], (L, B_DECAY_LORA, B_WIDTH), 0.5 * B_DECAY_LORA ** -0.5),
        "b_a0": nrm(ks[10], (L, B_WIDTH), 0.1),
        "b_a2": nrm(ks[11], (L, B_ICLR_LORA, B_WIDTH), 0.5 * B_ICLR_LORA ** -0.5),
        "b_g2": nrm(ks[12], (L, B_GATE_LORA, B_WIDTH), B_GATE_LORA ** -0.5),
        "b_k_k": 0.85 + nrm(ks[13], (L, B_WIDTH), 0.01),
        "b_k_a": 1.0 + nrm(ks[14], (L, B_WIDTH), 0.01),
        "b_r_k": nrm(ks[15], (L, B_HEADS, B_HEAD_DIM), 0.1),
        "b_ln_w": 1.0 + nrm(ks[16], (L, B_WIDTH), 0.01),
        "b_ln_b": nrm(ks[17], (L, B_WIDTH), 0.01),
        "w_up_a": nrm(ks[18], (L, A_WIDTH, D_MODEL), A_WIDTH ** -0.5),
        "w_up_b": nrm(ks[19], (L, B_WIDTH, D_MODEL), B_WIDTH ** -0.5),
        "w_out": nrm(ks[20], (L, D_MODEL, D_MODEL), D_MODEL ** -0.5),
        "norm2_g": 1.0 + nrm(ks[21], (L, D_MODEL), 0.01),
        "w_router": nrm(ks[22], (L, D_MODEL, N_EXPERTS), D_MODEL ** -0.5),
        "b_router": nrm(ks[23], (L, N_EXPERTS), 0.01),
        "w1": nrm(ks[24], (L, N_EXPERTS, D_MODEL, 2 * D_EXPERT), D_MODEL ** -0.5),
        "b1": nrm(ks[25], (L, N_EXPERTS, 2 * D_EXPERT), 0.01),
        "w2": nrm(ks[26], (L, N_EXPERTS, D_EXPERT, D_MODEL), D_EXPERT ** -0.5),
        "b2": nrm(ks[27], (L, N_EXPERTS, D_MODEL), 0.01),
        "final_g": 1.0 + nrm(ks[28], (D_MODEL,), 0.01),
    }


def reference(x, norm1_g, w_in, a_conv_w, a_A_log, a_dt_bias, a_norm_g, b_mu, b_w0, b_w2, b_a0, b_a2,
              b_g2, b_k_k, b_k_a, b_r_k, b_ln_w, b_ln_b, w_up_a, w_up_b, w_out, norm2_g, w_router,
              b_router, w1, b1, w2, b2, final_g):
    for l in range(DEPTH):
        h = rmsnorm(x, norm1_g[l])
        p = h @ w_in[l]
        pa = p[..., :A_COLS]
        pb = p[..., A_COLS:A_COLS + B_COLS]
        pg = p[..., A_COLS + B_COLS:]
        y_a = mixer_a(pa, a_conv_w[l], a_A_log[l], a_dt_bias[l], a_norm_g[l]) @ w_up_a[l]
        y_b = mixer_b(pb, b_mu[l], b_w0[l], b_w2[l], b_a0[l], b_a2[l], b_g2[l], b_k_k[l], b_k_a[l],
                      b_r_k[l], b_ln_w[l], b_ln_b[l]) @ w_up_b[l]
        merged = jax.nn.sigmoid(pg[..., :D_MODEL]) * y_a + jax.nn.sigmoid(pg[..., D_MODEL:]) * y_b
        x = x + merged @ w_out[l]
        h = rmsnorm(x, norm2_g[l])
        x = x + moe_ffn(h, w_router[l], b_router[l], w1[l], b1[l], w2[l], b2[l])
    return rmsnorm(x, final_g)
```

```python
import functools

import jax
import jax.numpy as jnp
from jax import lax
from jax.experimental import pallas as pl
from jax.experimental.pallas import tpu as pltpu

F32 = jnp.float32
BF16 = jnp.bfloat16
I32 = jnp.int32

A_HEADS = 4
A_HEAD_DIM = 128
A_WIDTH = A_HEADS * A_HEAD_DIM
A_CONV = 4
B_HEADS = 8
B_HEAD_DIM = 64
B_WIDTH = B_HEADS * B_HEAD_DIM
B_DECAY_LORA = 64
B_ICLR_LORA = 64
B_GATE_LORA = 128
B_GN_EPS = 64e-5
N_EXPERTS = 32
TOP_K = 4
SWIGLU_ALPHA = 1.702
SWIGLU_LIMIT = 7.0
RMS_EPS = 1e-5
L2_EPS = 1e-6

LANES = 128
SUBLANES = 8
INV_BLOCK = 16
A_CHUNK = 128
B_CHUNK = 64
VMEM_LIMIT = 56 * 1024 * 1024


def _dot(a, b):
    return jnp.dot(a.astype(BF16), b.astype(BF16), preferred_element_type=F32)


def _dot_nt(a, b):
    return lax.dot_general(a.astype(BF16), b.astype(BF16), (((1,), (1,)), ((), ())),
                           preferred_element_type=F32)


def _dot_tn(a, b):
    return lax.dot_general(a.astype(BF16), b.astype(BF16), (((0,), (0,)), ((), ())),
                           preferred_element_type=F32)


def _dot_f32(a, b):
    return jnp.dot(a, b, preferred_element_type=F32, precision=lax.Precision.HIGHEST)


def _sigmoid(x):
    return 1.0 / (1.0 + jnp.exp(-x))


def _silu(x):
    return x * _sigmoid(x)


def _softplus(x):
    return jnp.maximum(x, 0.0) + jnp.log(1.0 + jnp.exp(-jnp.abs(x)))


def _rmsnorm(x, g):
    return x * lax.rsqrt(jnp.mean(x * x, axis=-1, keepdims=True) + RMS_EPS) * g


def _iota2(shape, dim):
    return lax.broadcasted_iota(I32, shape, dim)


def _unit_lower_inverse(a, unit):
    n = a.shape[0]
    row = _iota2((n, n), 0)
    col = _iota2((n, n), 1)
    eye = (row == col).astype(F32)
    same_blk = (row // INV_BLOCK) == (col // INV_BLOCK)
    x = jnp.where(same_blk, -a, 0.0)
    dinv = eye + x
    p = x
    for _ in range(3):
        p = _dot_f32(p, p)
        dinv = dinv + _dot_f32(dinv, p)
    y = -_dot_f32(dinv, jnp.where(same_blk, 0.0, a))
    t = eye + y
    p = y
    steps = unit // INV_BLOCK
    k = 2
    while k < steps:
        p = _dot_f32(p, p)
        t = t + _dot_f32(t, p)
        k *= 2
    return _dot_f32(t, dinv)


def _gdn_kernel(x_ref, g1_ref, wa_ref, convw_ref, apar_ref, ng_ref, oa_ref,
                ext_ref, s_ref, q_s, k_s, v_s, o_s, beta_s, gc_s, *, tt, chunk):
    qkv_w = 3 * A_WIDTH

    @pl.when(pl.program_id(1) == 0)
    def _():
        ext_ref[0:SUBLANES, :] = jnp.zeros((SUBLANES, qkv_w), F32)
        s_ref[...] = jnp.zeros(s_ref.shape, F32)

    h = _rmsnorm(x_ref[...], g1_ref[...]).astype(BF16)
    pa = jnp.dot(h, wa_ref[...], preferred_element_type=F32)

    ext_ref[SUBLANES:SUBLANES + tt, :] = pa[:, :qkv_w]
    conv = jnp.zeros((tt, qkv_w), F32)
    for j in range(A_CONV):
        off = SUBLANES - (A_CONV - 1) + j
        conv = conv + convw_ref[j:j + 1, :] * ext_ref[off:off + tt, :]
    ext_ref[0:SUBLANES, :] = ext_ref[tt:tt + SUBLANES, :]
    qkv = _silu(conv)

    for hd in range(A_HEADS):
        lo = hd * A_HEAD_DIM
        qh = qkv[:, lo:lo + A_HEAD_DIM]
        kh = qkv[:, A_WIDTH + lo:A_WIDTH + lo + A_HEAD_DIM]
        qn = qh * lax.rsqrt(jnp.sum(qh * qh, axis=-1, keepdims=True) + L2_EPS) * (A_HEAD_DIM ** -0.5)
        kn = kh * lax.rsqrt(jnp.sum(kh * kh, axis=-1, keepdims=True) + L2_EPS)
        q_s[:, lo:lo + A_HEAD_DIM] = qn
        k_s[:, lo:lo + A_HEAD_DIM] = kn
    v_s[...] = qkv[:, 2 * A_WIDTH:]

    bg = pa[:, 4 * A_WIDTH:]
    beta_s[...] = _sigmoid(bg)
    g_all = apar_ref[0:1, :] * _softplus(bg + apar_ref[1:2, :])
    ri = _iota2((chunk, chunk), 0)
    ci = _iota2((chunk, chunk), 1)
    causal = ci <= ri
    strict = ci < ri
    tri = causal.astype(F32)
    for c in range(tt // chunk):
        gc_s[c * chunk:(c + 1) * chunk, :] = _dot_f32(tri, g_all[c * chunk:(c + 1) * chunk, :])

    def chunk_step(c, carry):
        r0 = pl.multiple_of(c * chunk, chunk)
        gcc = gc_s[pl.ds(r0, chunk), :]
        gct = gcc.T
        bc = beta_s[pl.ds(r0, chunk), :]
        for hd in range(A_HEADS):
            lo = hd * A_HEAD_DIM
            q = q_s[pl.ds(r0, chunk), lo:lo + A_HEAD_DIM]
            k = k_s[pl.ds(r0, chunk), lo:lo + A_HEAD_DIM]
            v = v_s[pl.ds(r0, chunk), lo:lo + A_HEAD_DIM]
            beta = bc[:, hd:hd + 1]
            gcol = gcc[:, A_HEADS + hd:A_HEADS + hd + 1]
            grow = gct[A_HEADS + hd:A_HEADS + hd + 1, :]
            glast = gcc[chunk - 1:chunk, A_HEADS + hd:A_HEADS + hd + 1]
            decay = jnp.where(causal, jnp.exp(jnp.where(causal, gcol - grow, 0.0)), 0.0)
            eg = jnp.exp(gcol)
            kb = k * beta
            lmat = jnp.where(strict, _dot_nt(kb, k) * decay, 0.0)
            tmat = _unit_lower_inverse(lmat, chunk)
            u = _dot(tmat, v * beta)
            w = _dot(tmat, kb * eg)
            attn = _dot_nt(q, k) * decay
            s = s_ref[hd]
            v_new = u - _dot(w, s)
            o = _dot(q * eg, s) + _dot(attn, v_new)
            s_ref[hd] = s * jnp.exp(glast) + _dot_tn(k * jnp.exp(glast - gcol), v_new)
            o_s[pl.ds(r0, chunk), lo:lo + A_HEAD_DIM] = o
        return carry

    lax.fori_loop(0, tt // chunk, chunk_step, 0)

    z = pa[:, 3 * A_WIDTH:4 * A_WIDTH]
    for hd in range(A_HEADS):
        lo = hd * A_HEAD_DIM
        o = o_s[:, lo:lo + A_HEAD_DIM]
        o = o * lax.rsqrt(jnp.mean(o * o, axis=-1, keepdims=True) + L2_EPS) * ng_ref[...]
        oa_ref[:, lo:lo + A_HEAD_DIM] = (o * _silu(z[:, lo:lo + A_HEAD_DIM])).astype(oa_ref.dtype)


def _gdn(x2d, norm1_g, w_a, conv_w, a_par, norm_g, *, batch, seq, tt, chunk):
    n, d = x2d.shape
    nt = seq // tt
    cols = w_a.shape[1]
    kern = functools.partial(_gdn_kernel, tt=tt, chunk=chunk)
    const = lambda b, t: (0, 0)
    return pl.pallas_call(
        kern,
        out_shape=jax.ShapeDtypeStruct((n, A_WIDTH), BF16),
        grid=(batch, nt),
        in_specs=[
            pl.BlockSpec((tt, d), lambda b, t: (b * nt + t, 0)),
            pl.BlockSpec((1, d), const),
            pl.BlockSpec((d, cols), const),
            pl.BlockSpec((A_CONV, 3 * A_WIDTH), const),
            pl.BlockSpec((SUBLANES, LANES), const),
            pl.BlockSpec((1, A_HEAD_DIM), const),
        ],
        out_specs=pl.BlockSpec((tt, A_WIDTH), lambda b, t: (b * nt + t, 0)),
        scratch_shapes=[
            pltpu.VMEM((tt + SUBLANES, 3 * A_WIDTH), F32),
            pltpu.VMEM((A_HEADS, A_HEAD_DIM, A_HEAD_DIM), F32),
            pltpu.VMEM((tt, A_WIDTH), F32),
            pltpu.VMEM((tt, A_WIDTH), F32),
            pltpu.VMEM((tt, A_WIDTH), F32),
            pltpu.VMEM((tt, A_WIDTH), F32),
            pltpu.VMEM((tt, LANES), F32),
            pltpu.VMEM((tt, LANES), F32),
        ],
        compiler_params=pltpu.CompilerParams(
            dimension_semantics=("parallel", "arbitrary"), vmem_limit_bytes=VMEM_LIMIT),
        name="gdn",
    )(x2d, norm1_g, w_a, conv_w, a_par, norm_g)


def _rwkv_kernel(x_ref, g1_ref, wb_ref, mu_ref, wlora_ref, g2_ref, vec_ref, hsum_ref, ob_ref,
                 ext_ref, s_ref, at_s, kt_s, bt_s, rt_s, kh_s, bh_s, v_s, o_s, wtot_s, *, tt, chunk):
    bcols = 3 * B_WIDTH + B_DECAY_LORA + B_ICLR_LORA + B_GATE_LORA
    npairs = B_WIDTH // LANES

    @pl.when(pl.program_id(1) == 0)
    def _():
        ext_ref[0:SUBLANES, :] = jnp.zeros((SUBLANES, bcols), F32)
        s_ref[...] = jnp.zeros(s_ref.shape, F32)

    h = _rmsnorm(x_ref[...], g1_ref[...]).astype(BF16)
    pb = jnp.dot(h, wb_ref[...], preferred_element_type=F32)

    ext_ref[SUBLANES:SUBLANES + tt, :] = pb
    prev = ext_ref[SUBLANES - 1:SUBLANES - 1 + tt, :]
    ext_ref[0:SUBLANES, :] = ext_ref[tt:tt + SUBLANES, :]
    xm = pb + (prev - pb) * mu_ref[...]

    r = xm[:, 0:B_WIDTH]
    k = xm[:, B_WIDTH:2 * B_WIDTH]
    v = xm[:, 2 * B_WIDTH:3 * B_WIDTH]
    xwa = xm[:, 3 * B_WIDTH:3 * B_WIDTH + LANES]
    xg = xm[:, 3 * B_WIDTH + LANES:]
    lane = _iota2((tt, LANES), 1)
    lora_in = jnp.where(lane < B_DECAY_LORA, jnp.tanh(xwa), xwa)
    lora = _dot_f32(lora_in, wlora_ref[...])
    w0 = vec_ref[0:1, :]
    a0 = vec_ref[1:2, :]
    k_k = vec_ref[2:3, :]
    k_a = vec_ref[3:4, :]
    r_k = vec_ref[4:5, :]
    ln_w = vec_ref[5:6, :]
    ln_b = vec_ref[6:7, :]
    wl = -_softplus(-(w0 + lora[:, :B_WIDTH])) - 0.5
    lw = -jnp.exp(wl)
    a = _sigmoid(a0 + lora[:, B_WIDTH:])
    gate = _dot_f32(_sigmoid(xg), g2_ref[...])

    hsum = hsum_ref[...]

    def head_sum(t):
        hi = t.astype(BF16)
        lo = (t - hi.astype(F32)).astype(BF16)
        return (jnp.dot(hi, hsum, preferred_element_type=F32)
                + jnp.dot(lo, hsum, preferred_element_type=F32))

    kk = k * k_k
    kk = kk * lax.rsqrt(head_sum(kk * kk) + L2_EPS)
    kf = k * (1.0 + (a - 1.0) * k_a)
    beta = kk * a

    tri = (_iota2((chunk, chunk), 1) <= _iota2((chunk, chunk), 0)).astype(F32)
    for c in range(tt // chunk):
        rs = slice(c * chunk, (c + 1) * chunk)
        lwc = lw[rs]
        gcum = _dot_f32(tri, lwc)
        gtot = gcum[chunk - 1:chunk, :]
        w_inv = jnp.exp(-gcum)
        w_rem = jnp.exp(gtot - gcum)
        at_s[rs, :] = kk[rs] * jnp.exp(gcum - lwc)
        kt_s[rs, :] = kf[rs] * w_inv
        bt_s[rs, :] = beta[rs] * w_inv
        rt_s[rs, :] = r[rs] * jnp.exp(gcum)
        kh_s[rs, :] = kf[rs] * w_rem
        bh_s[rs, :] = beta[rs] * w_rem
        wtot_s[c * SUBLANES:(c + 1) * SUBLANES, :] = jnp.broadcast_to(jnp.exp(gtot), (SUBLANES, B_WIDTH))
    v_s[...] = v

    n2 = 2 * chunk
    ri = _iota2((n2, n2), 0)
    ci = _iota2((n2, n2), 1)
    incl = ci <= ri
    strict = ci < ri
    lane2 = _iota2((chunk, LANES), 1)
    m0 = lane2 < B_HEAD_DIM

    def stack(t):
        return jnp.concatenate([jnp.where(m0, t, 0.0), jnp.where(m0, 0.0, t)], axis=0)

    def chunk_step(c, carry):
        r0 = pl.multiple_of(c * chunk, chunk)
        for p in range(npairs):
            lo = p * LANES
            sl = (pl.ds(r0, chunk), slice(lo, lo + LANES))
            at = stack(at_s[sl])
            kt = stack(kt_s[sl])
            bt = stack(bt_s[sl])
            rt = stack(rt_s[sl])
            kh = stack(kh_s[sl])
            bh = stack(bh_s[sl])
            vs = stack(v_s[sl])
            w0s = pl.multiple_of(c * SUBLANES, SUBLANES)
            wtot = wtot_s[pl.ds(w0s, SUBLANES), lo:lo + LANES][0:1, :]
            a_ak = jnp.where(strict, _dot_nt(at, kt), 0.0)
            a_ab = jnp.where(strict, _dot_nt(at, bt), 0.0)
            a_rk = jnp.where(incl, _dot_nt(rt, kt), 0.0)
            a_rb = jnp.where(incl, _dot_nt(rt, bt), 0.0)
            tmat = _unit_lower_inverse(a_ab, chunk)
            s = s_ref[p]
            u = _dot(tmat, _dot_nt(at, s) + _dot(a_ak, vs))
            o2 = _dot_nt(rt, s) + _dot(a_rk, vs) - _dot(a_rb, u)
            s_ref[p] = s * wtot + _dot_tn(vs, kh) - _dot_tn(u, bh)
            o_s[sl] = o2[:chunk] + o2[chunk:]
        return carry

    lax.fori_loop(0, tt // chunk, chunk_step, 0)

    out = o_s[...]
    mean = head_sum(out) * (1.0 / B_HEAD_DIM)
    cen = out - mean
    var = head_sum(cen * cen) * (1.0 / B_HEAD_DIM)
    out = cen * lax.rsqrt(var + B_GN_EPS) * ln_w + ln_b
    bonus = head_sum(r * kf * r_k) * v
    ob_ref[...] = ((out + bonus) * gate).astype(ob_ref.dtype)


def _rwkv(x2d, norm1_g, w_b, mu, w_lora, g2, vecs, hsum, *, batch, seq, tt, chunk):
    n, d = x2d.shape
    nt = seq // tt
    bcols = w_b.shape[1]
    kern = functools.partial(_rwkv_kernel, tt=tt, chunk=chunk)
    const = lambda b, t: (0, 0)
    big = lambda: pltpu.VMEM((tt, B_WIDTH), F32)
    return pl.pallas_call(
        kern,
        out_shape=jax.ShapeDtypeStruct((n, B_WIDTH), BF16),
        grid=(batch, nt),
        in_specs=[
            pl.BlockSpec((tt, d), lambda b, t: (b * nt + t, 0)),
            pl.BlockSpec((1, d), const),
            pl.BlockSpec((d, bcols), const),
            pl.BlockSpec((1, bcols), const),
            pl.BlockSpec((LANES, 2 * B_WIDTH), const),
            pl.BlockSpec((B_GATE_LORA, B_WIDTH), const),
            pl.BlockSpec((SUBLANES, B_WIDTH), const),
            pl.BlockSpec((B_WIDTH, B_WIDTH), const),
        ],
        out_specs=pl.BlockSpec((tt, B_WIDTH), lambda b, t: (b * nt + t, 0)),
        scratch_shapes=[
            pltpu.VMEM((tt + SUBLANES, bcols), F32),
            pltpu.VMEM((B_WIDTH // LANES, LANES, LANES), F32),
            big(), big(), big(), big(), big(), big(), big(), big(),
            pltpu.VMEM((tt // chunk * SUBLANES, B_WIDTH), F32),
        ],
        compiler_params=pltpu.CompilerParams(
            dimension_semantics=("parallel", "arbitrary"), vmem_limit_bytes=VMEM_LIMIT),
        name="rwkv",
    )(x2d, norm1_g, w_b, mu, w_lora, g2, vecs, hsum)


def _merge_kernel(x_ref, g1_ref, oa_ref, ob_ref, wg_ref, wua_ref, wub_ref, wo_ref, x2_ref):
    d = x_ref.shape[1]
    x = x_ref[...]
    h = _rmsnorm(x, g1_ref[...]).astype(BF16)
    pg = jnp.dot(h, wg_ref[...], preferred_element_type=F32)
    ya = jnp.dot(oa_ref[...], wua_ref[...], preferred_element_type=F32)
    yb = jnp.dot(ob_ref[...], wub_ref[...], preferred_element_type=F32)
    merged = _sigmoid(pg[:, :d]) * ya + _sigmoid(pg[:, d:]) * yb
    x2_ref[...] = x + jnp.dot(merged.astype(BF16), wo_ref[...], preferred_element_type=F32)


def _merge(x2d, norm1_g, oa, ob, w_g, w_up_a, w_up_b, w_out, *, tm):
    n, d = x2d.shape
    const = lambda i: (0, 0)
    rows = lambda i: (i, 0)
    return pl.pallas_call(
        _merge_kernel,
        out_shape=jax.ShapeDtypeStruct((n, d), F32),
        grid=(n // tm,),
        in_specs=[
            pl.BlockSpec((tm, d), rows),
            pl.BlockSpec((1, d), const),
            pl.BlockSpec((tm, A_WIDTH), rows),
            pl.BlockSpec((tm, B_WIDTH), rows),
            pl.BlockSpec((d, 2 * d), const),
            pl.BlockSpec((A_WIDTH, d), const),
            pl.BlockSpec((B_WIDTH, d), const),
            pl.BlockSpec((d, d), const),
        ],
        out_specs=pl.BlockSpec((tm, d), rows),
        compiler_params=pltpu.CompilerParams(
            dimension_semantics=("parallel",), vmem_limit_bytes=VMEM_LIMIT),
        name="merge",
    )(x2d, norm1_g, oa, ob, w_g, w_up_a, w_up_b, w_out)


def _router_kernel(x2_ref, g2_ref, wr_ref, br_ref, h2_ref, meta_ref, cnt_ref, run_ref):
    tm = x2_ref.shape[0]

    @pl.when(pl.program_id(0) == 0)
    def _():
        run_ref[...] = jnp.zeros(run_ref.shape, F32)

    h2 = _rmsnorm(x2_ref[...], g2_ref[...])
    h2_ref[...] = h2
    logits = _dot_f32(h2, wr_ref[...]) + br_ref[...]
    lane = _iota2((tm, LANES), 1)
    neg = jnp.float32(-jnp.inf)

    work = logits
    sel_idx = []
    sel_val = []
    multi = jnp.zeros((tm, LANES), F32)
    for _ in range(TOP_K):
        mx = jnp.max(work, axis=-1, keepdims=True)
        idx = jnp.min(jnp.where(work == mx, lane, LANES), axis=-1, keepdims=True)
        hit = lane == idx
        multi = multi + hit.astype(F32)
        work = jnp.where(hit, neg, work)
        sel_idx.append(idx)
        sel_val.append(mx)
    ex = [jnp.exp(vv - sel_val[0]) for vv in sel_val]
    den = ex[0] + ex[1] + ex[2] + ex[3]
    gates = [e / den for e in ex]

    row = _iota2((tm, tm), 0)
    col = _iota2((tm, tm), 1)
    before = (col < row).astype(BF16)
    excl = jnp.dot(before, multi.astype(BF16), preferred_element_type=F32) + run_ref[0:1, :]
    run_ref[0:1, :] = run_ref[0:1, :] + jnp.sum(multi, axis=0, keepdims=True)
    cnt_ref[...] = run_ref[...]

    meta = jnp.zeros((tm, LANES), F32)
    for kk in range(TOP_K):
        rank = jnp.sum(jnp.where(lane == sel_idx[kk], excl, 0.0), axis=-1, keepdims=True)
        meta = jnp.where(lane == kk, sel_idx[kk].astype(F32), meta)
        meta = jnp.where(lane == TOP_K + kk, rank, meta)
        meta = jnp.where(lane == 2 * TOP_K + kk, gates[kk], meta)
    meta_ref[...] = meta


def _router(x2, norm2_g, w_r, b_r, *, tm):
    n, d = x2.shape
    const = lambda i: (0, 0)
    rows = lambda i: (i, 0)
    return pl.pallas_call(
        _router_kernel,
        out_shape=(jax.ShapeDtypeStruct((n, d), F32),
                   jax.ShapeDtypeStruct((n, LANES), F32),
                   jax.ShapeDtypeStruct((SUBLANES, LANES), F32)),
        grid=(n // tm,),
        in_specs=[
            pl.BlockSpec((tm, d), rows),
            pl.BlockSpec((1, d), const),
            pl.BlockSpec((d, LANES), const),
            pl.BlockSpec((1, LANES), const),
        ],
        out_specs=(pl.BlockSpec((tm, d), rows),
                   pl.BlockSpec((tm, LANES), rows),
                   pl.BlockSpec((SUBLANES, LANES), const)),
        scratch_shapes=[pltpu.VMEM((SUBLANES, LANES), F32)],
        compiler_params=pltpu.CompilerParams(
            dimension_semantics=("arbitrary",), vmem_limit_bytes=VMEM_LIMIT),
        name="router",
    )(x2, norm2_g, w_r, b_r)


def _dispatch_kernel(pos_ref, h2_ref, zero_ref, xs_ref, sem):
    del zero_ref
    tm = h2_ref.shape[0]
    base = pl.program_id(0) * tm * TOP_K

    def row_copy(t, kk):
        dst = pos_ref[base + t * TOP_K + kk]
        return pltpu.make_async_copy(h2_ref.at[pl.ds(t, 1), :], xs_ref.at[pl.ds(dst, 1), :], sem)

    def issue(t, carry):
        for kk in range(TOP_K):
            row_copy(t, kk).start()
        return carry

    def drain(t, carry):
        for kk in range(TOP_K):
            row_copy(t, kk).wait()
        return carry

    lax.fori_loop(0, tm, issue, 0)
    lax.fori_loop(0, tm, drain, 0)


def _dispatch(pos, h2, zeros, *, tm):
    n, d = h2.shape
    p_rows = zeros.shape[0]
    return pl.pallas_call(
        _dispatch_kernel,
        out_shape=jax.ShapeDtypeStruct((p_rows, d), F32),
        grid_spec=pltpu.PrefetchScalarGridSpec(
            num_scalar_prefetch=1,
            grid=(n // tm,),
            in_specs=[pl.BlockSpec((tm, d), lambda i, pos: (i, 0)),
                      pl.BlockSpec(memory_space=pl.ANY)],
            out_specs=pl.BlockSpec(memory_space=pl.ANY),
            scratch_shapes=[pltpu.SemaphoreType.DMA],
        ),
        input_output_aliases={2: 0},
        compiler_params=pltpu.CompilerParams(
            dimension_semantics=("arbitrary",), vmem_limit_bytes=VMEM_LIMIT, has_side_effects=True),
        name="dispatch",
    )(pos, h2, zeros)


def _experts_kernel(te_ref, nu_ref, xs_ref, w1_ref, b1_ref, w2_ref, b2_ref, y_ref):
    del te_ref
    de = w2_ref.shape[1]

    @pl.when(pl.program_id(0) < nu_ref[0])
    def _():
        x = xs_ref[...].astype(BF16)
        h1 = jnp.dot(x, w1_ref[0], preferred_element_type=F32) + b1_ref[0]
        glu = jnp.minimum(h1[:, :de], SWIGLU_LIMIT)
        lin = jnp.clip(h1[:, de:], -SWIGLU_LIMIT, SWIGLU_LIMIT)
        act = glu * _sigmoid(SWIGLU_ALPHA * glu) * (lin + 1.0)
        y_ref[...] = jnp.dot(act.astype(BF16), w2_ref[0], preferred_element_type=F32) + b2_ref[0]

    @pl.when(pl.program_id(0) >= nu_ref[0])
    def _():
        y_ref[...] = jnp.zeros(y_ref.shape, y_ref.dtype)


def _experts(tile_expert, n_used, xs, w1, b1, w2, b2, *, tm):
    p_rows, d = xs.shape
    ne, _, de2 = w1.shape
    de = w2.shape[1]
    n_tiles = p_rows // tm
    wsel = lambda i, te, nu: (te[i], 0, 0)
    rows = lambda i, te, nu: (i, 0)
    return pl.pallas_call(
        _experts_kernel,
        out_shape=jax.ShapeDtypeStruct((p_rows, d), F32),
        grid_spec=pltpu.PrefetchScalarGridSpec(
            num_scalar_prefetch=2,
            grid=(n_tiles,),
            in_specs=[pl.BlockSpec((tm, d), rows),
                      pl.BlockSpec((1, d, de2), wsel),
                      pl.BlockSpec((1, 1, de2), wsel),
                      pl.BlockSpec((1, de, d), wsel),
                      pl.BlockSpec((1, 1, d), wsel)],
            out_specs=pl.BlockSpec((tm, d), rows),
        ),
        compiler_params=pltpu.CompilerParams(
            dimension_semantics=("arbitrary",), vmem_limit_bytes=VMEM_LIMIT),
        name="experts",
    )(tile_expert, n_used, xs, w1, b1, w2, b2)


def _combine_kernel(pos_ref, x2_ref, meta_ref, fg_ref, y_ref, out_ref, buf, sem, *, final_norm):
    tm = x2_ref.shape[0]
    base = pl.program_id(0) * tm * TOP_K

    def row_copy(t, kk):
        src = pos_ref[base + t * TOP_K + kk]
        return pltpu.make_async_copy(y_ref.at[pl.ds(src, 1), :], buf.at[kk, pl.ds(t, 1), :], sem)

    def issue(t, carry):
        for kk in range(TOP_K):
            row_copy(t, kk).start()
        return carry

    def drain(t, carry):
        for kk in range(TOP_K):
            row_copy(t, kk).wait()
        return carry

    lax.fori_loop(0, tm, issue, 0)
    lax.fori_loop(0, tm, drain, 0)

    acc = x2_ref[...]
    meta = meta_ref[...]
    for kk in range(TOP_K):
        acc = acc + buf[kk] * meta[:, 2 * TOP_K + kk:2 * TOP_K + kk + 1]
    out_ref[...] = _rmsnorm(acc, fg_ref[...]) if final_norm else acc


def _combine(pos, x2, meta, final_g, y, *, tm, final_norm):
    n, d = x2.shape
    return pl.pallas_call(
        functools.partial(_combine_kernel, final_norm=final_norm),
        out_shape=jax.ShapeDtypeStruct((n, d), F32),
        grid_spec=pltpu.PrefetchScalarGridSpec(
            num_scalar_prefetch=1,
            grid=(n // tm,),
            in_specs=[pl.BlockSpec((tm, d), lambda i, pos: (i, 0)),
                      pl.BlockSpec((tm, LANES), lambda i, pos: (i, 0)),
                      pl.BlockSpec((1, d), lambda i, pos: (0, 0)),
                      pl.BlockSpec(memory_space=pl.ANY)],
            out_specs=pl.BlockSpec((tm, d), lambda i, pos: (i, 0)),
            scratch_shapes=[pltpu.VMEM((TOP_K, tm, d), F32), pltpu.SemaphoreType.DMA],
        ),
        compiler_params=pltpu.CompilerParams(
            dimension_semantics=("arbitrary",), vmem_limit_bytes=VMEM_LIMIT),
        name="combine",
    )(pos, x2, meta, final_g, y)


def _pad_cols(w, cols):
    return jnp.pad(w, ((0, 0), (0, cols - w.shape[1])))


def _layer(x2d, p, *, batch, seq, tiles):
    d = x2d.shape[1]
    a_cols = 4 * A_WIDTH + 2 * A_HEADS
    b_cols = 3 * B_WIDTH + B_DECAY_LORA + B_ICLR_LORA + B_GATE_LORA
    w_in = p["w_in"]
    w_a = _pad_cols(w_in[:, :a_cols], 4 * A_WIDTH + LANES).astype(BF16)
    w_b = w_in[:, a_cols:a_cols + b_cols].astype(BF16)
    w_g = w_in[:, a_cols + b_cols:].astype(BF16)
    g1 = p["norm1_g"].reshape(1, d)

    a_par = jnp.zeros((SUBLANES, LANES), F32)
    a_par = a_par.at[0, A_HEADS:2 * A_HEADS].set(-jnp.exp(p["a_A_log"]))
    a_par = a_par.at[1, A_HEADS:2 * A_HEADS].set(p["a_dt_bias"])
    oa = _gdn(x2d, g1, w_a, p["a_conv_w"], a_par, p["a_norm_g"].reshape(1, A_HEAD_DIM),
              batch=batch, seq=seq, tt=tiles["tt_a"], chunk=A_CHUNK)

    w_lora = jnp.zeros((LANES, 2 * B_WIDTH), F32)
    w_lora = w_lora.at[:B_DECAY_LORA, :B_WIDTH].set(p["b_w2"])
    w_lora = w_lora.at[B_DECAY_LORA:, B_WIDTH:].set(p["b_a2"])
    vecs = jnp.stack([p["b_w0"], p["b_a0"], p["b_k_k"], p["b_k_a"], p["b_r_k"].reshape(-1),
                      p["b_ln_w"], p["b_ln_b"], jnp.zeros((B_WIDTH,), F32)])
    hd = jnp.arange(B_WIDTH) // B_HEAD_DIM
    hsum = (hd[:, None] == hd[None, :]).astype(BF16)
    ob = _rwkv(x2d, g1, w_b, p["b_mu"].reshape(1, b_cols), w_lora, p["b_g2"], vecs, hsum,
               batch=batch, seq=seq, tt=tiles["tt_b"], chunk=B_CHUNK)

    x2 = _merge(x2d, g1, oa, ob, w_g, p["w_up_a"].astype(BF16), p["w_up_b"].astype(BF16),
                p["w_out"].astype(BF16), tm=tiles["tm_merge"])

    w_r = _pad_cols(p["w_router"], LANES)
    b_r = jnp.full((1, LANES), -jnp.inf, F32).at[0, :N_EXPERTS].set(p["b_router"])
    h2, meta, cnt = _router(x2, p["norm2_g"].reshape(1, d), w_r, b_r, tm=tiles["tm_router"])

    tm_e = tiles["tm_expert"]
    n = x2d.shape[0]
    counts = cnt[0, :N_EXPERTS].astype(I32)
    tiles_per = (counts + tm_e - 1) // tm_e
    tile_end = jnp.cumsum(tiles_per)
    pad_off = (tile_end - tiles_per) * tm_e
    n_tiles = (n * TOP_K) // tm_e + N_EXPERTS
    n_used = tile_end[-1:]
    tile_ids = jnp.arange(n_tiles, dtype=I32)
    tile_expert = jnp.minimum(jnp.searchsorted(tile_end, tile_ids, side="right"), N_EXPERTS - 1).astype(I32)
    last_e = tile_expert[jnp.maximum(n_used[0] - 1, 0)]
    tile_expert = jnp.where(tile_ids < n_used[0], tile_expert, last_e)
    idx = meta[:, :TOP_K].astype(I32)
    rank = meta[:, TOP_K:2 * TOP_K].astype(I32)
    pos = (pad_off[idx] + rank).reshape(-1)

    xs = _dispatch(pos, h2, jnp.zeros((n_tiles * tm_e, d), F32), tm=tiles["tm_dispatch"])
    y = _experts(tile_expert, n_used.astype(I32), xs, p["w1"].astype(BF16),
                 p["b1"].reshape(N_EXPERTS, 1, -1), p["w2"].astype(BF16),
                 p["b2"].reshape(N_EXPERTS, 1, -1), tm=tm_e)
    return pos, x2, meta, y


def _tiles(seq, n):
    return dict(tt_a=min(seq, 512), tt_b=min(seq, 512), tm_merge=min(n, 512), tm_router=min(n, 512),
                tm_expert=min(n, 512), tm_dispatch=min(n, 512), tm_combine=min(n, 256))


def kernel(x, norm1_g, w_in, a_conv_w, a_A_log, a_dt_bias, a_norm_g, b_mu, b_w0, b_w2, b_a0, b_a2, b_g2, b_k_k, b_k_a, b_r_k, b_ln_w, b_ln_b, w_up_a, w_up_b, w_out, norm2_g, w_router, b_router, w1, b1, w2, b2, final_g):
    batch, seq, d = x.shape
    n = batch * seq
    depth = norm1_g.shape[0]
    tiles = _tiles(seq, n)
    params = dict(norm1_g=norm1_g, w_in=w_in, a_conv_w=a_conv_w, a_A_log=a_A_log, a_dt_bias=a_dt_bias,
                  a_norm_g=a_norm_g, b_mu=b_mu, b_w0=b_w0, b_w2=b_w2, b_a0=b_a0, b_a2=b_a2, b_g2=b_g2,
                  b_k_k=b_k_k, b_k_a=b_k_a, b_r_k=b_r_k, b_ln_w=b_ln_w, b_ln_b=b_ln_b, w_up_a=w_up_a,
                  w_up_b=w_up_b, w_out=w_out, norm2_g=norm2_g, w_router=w_router, b_router=b_router,
                  w1=w1, b1=b1, w2=w2, b2=b2)
    x2d = x.reshape(n, d)
    for l in range(depth):
        p = {k: v[l] for k, v in params.items()}
        pos, x2, meta, y = _layer(x2d, p, batch=batch, seq=seq, tiles=tiles)
        x2d = _combine(pos, x2, meta, final_g.reshape(1, d), y, tm=tiles["tm_combine"],
                       final_norm=(l == depth - 1))
    return x2d.reshape(batch, seq, d)
```

```python
import functools

import jax
import jax.numpy as jnp
from jax import lax
from jax.experimental import pallas as pl
from jax.experimental.pallas import tpu as pltpu

F32 = jnp.float32
BF16 = jnp.bfloat16
I32 = jnp.int32

A_HEADS = 4
A_HEAD_DIM = 128
A_WIDTH = A_HEADS * A_HEAD_DIM
A_CONV = 4
B_HEADS = 8
B_HEAD_DIM = 64
B_WIDTH = B_HEADS * B_HEAD_DIM
B_DECAY_LORA = 64
B_ICLR_LORA = 64
B_GATE_LORA = 128
B_GN_EPS = 64e-5
N_EXPERTS = 32
TOP_K = 4
SWIGLU_ALPHA = 1.702
SWIGLU_LIMIT = 7.0
RMS_EPS = 1e-5
L2_EPS = 1e-6

LANES = 128
SUBLANES = 8
INV_BLOCK = 16
A_CHUNK = 128
B_CHUNK = 64
VMEM_LIMIT = 56 * 1024 * 1024


def _dot(a, b):
    return jnp.dot(a.astype(BF16), b.astype(BF16), preferred_element_type=F32)


def _dot_nt(a, b):
    return lax.dot_general(a.astype(BF16), b.astype(BF16), (((1,), (1,)), ((), ())),
                           preferred_element_type=F32)


def _dot_tn(a, b):
    return lax.dot_general(a.astype(BF16), b.astype(BF16), (((0,), (0,)), ((), ())),
                           preferred_element_type=F32)


def _dot_f32(a, b):
    return jnp.dot(a, b, preferred_element_type=F32, precision=lax.Precision.HIGHEST)


def _split_bf16(a, terms):
    out = []
    for _ in range(terms):
        hi = a.astype(BF16)
        out.append(hi)
        a = a - hi.astype(F32)
    return out


def _dot_x3(a, b):
    ah, al = _split_bf16(a, 2)
    bh, bl = _split_bf16(b, 2)
    mm = lambda u, v: jnp.dot(u, v, preferred_element_type=F32)
    return mm(ah, bh) + mm(ah, bl) + mm(al, bh)


def _dot_exact_lhs(a01, b):
    a01 = a01.astype(BF16)
    return sum(jnp.dot(a01, t, preferred_element_type=F32) for t in _split_bf16(b, 3))


def _sigmoid(x):
    return 1.0 / (1.0 + jnp.exp(-x))


def _silu(x):
    return x * _sigmoid(x)


def _softplus(x):
    return jnp.maximum(x, 0.0) + jnp.log(1.0 + jnp.exp(-jnp.abs(x)))


def _rmsnorm(x, g):
    return x * lax.rsqrt(jnp.mean(x * x, axis=-1, keepdims=True) + RMS_EPS) * g


def _iota2(shape, dim):
    return lax.broadcasted_iota(I32, shape, dim)


def _unit_lower_inverses(mats, unit):
    n = mats[0].shape[0]
    row = _iota2((n, n), 0)
    col = _iota2((n, n), 1)
    eye = (row == col).astype(F32)
    same_blk = (row // INV_BLOCK) == (col // INV_BLOCK)
    ps = [jnp.where(same_blk, -a, 0.0) for a in mats]
    dinv = [eye + x for x in ps]
    for _ in range(3):
        ps = [_dot(p, p) for p in ps]
        dinv = [dv + _dot(dv, p) for dv, p in zip(dinv, ps)]
    ps = [-_dot(dv, jnp.where(same_blk, 0.0, a)) for dv, a in zip(dinv, mats)]
    ts = [eye + y for y in ps]
    k = 2
    while k < unit // INV_BLOCK:
        ps = [_dot(p, p) for p in ps]
        ts = [t + _dot(t, p) for t, p in zip(ts, ps)]
        k *= 2
    return [_dot(t, dv) for t, dv in zip(ts, dinv)]


def _gdn_kernel(x_ref, g1_ref, wa_ref, convw_ref, apar_ref, ng_ref, oa_ref,
                ext_ref, s_ref, q_s, k_s, v_s, o_s, beta_s, gc_s, u_s, wq_s, attn_s, kg_s, egl_s, *, tt, chunk):
    qkv_w = 3 * A_WIDTH

    @pl.when(pl.program_id(1) == 0)
    def _():
        ext_ref[0:SUBLANES, :] = jnp.zeros((SUBLANES, qkv_w), F32)
        s_ref[...] = jnp.zeros(s_ref.shape, F32)

    h = _rmsnorm(x_ref[...], g1_ref[...]).astype(BF16)
    pa = jnp.dot(h, wa_ref[...], preferred_element_type=F32)

    ext_ref[SUBLANES:SUBLANES + tt, :] = pa[:, :qkv_w]
    conv = jnp.zeros((tt, qkv_w), F32)
    for j in range(A_CONV):
        off = SUBLANES - (A_CONV - 1) + j
        conv = conv + convw_ref[j:j + 1, :] * ext_ref[off:off + tt, :]
    ext_ref[0:SUBLANES, :] = ext_ref[tt:tt + SUBLANES, :]
    qkv = _silu(conv)

    for hd in range(A_HEADS):
        lo = hd * A_HEAD_DIM
        qh = qkv[:, lo:lo + A_HEAD_DIM]
        kh = qkv[:, A_WIDTH + lo:A_WIDTH + lo + A_HEAD_DIM]
        qn = qh * lax.rsqrt(jnp.sum(qh * qh, axis=-1, keepdims=True) + L2_EPS) * (A_HEAD_DIM ** -0.5)
        kn = kh * lax.rsqrt(jnp.sum(kh * kh, axis=-1, keepdims=True) + L2_EPS)
        q_s[:, lo:lo + A_HEAD_DIM] = qn
        k_s[:, lo:lo + A_HEAD_DIM] = kn
    v_s[...] = qkv[:, 2 * A_WIDTH:]

    bg = pa[:, 4 * A_WIDTH:]
    beta_s[...] = _sigmoid(bg)
    g_all = apar_ref[0:1, :] * _softplus(bg + apar_ref[1:2, :])
    ri = _iota2((chunk, chunk), 0)
    ci = _iota2((chunk, chunk), 1)
    causal = ci <= ri
    strict = ci < ri
    tri = causal.astype(F32)
    for c in range(tt // chunk):
        gc_s[c * chunk:(c + 1) * chunk, :] = _dot_exact_lhs(tri, g_all[c * chunk:(c + 1) * chunk, :])

    heads = range(A_HEADS)

    def prep_step(c, carry):
        r0 = pl.multiple_of(c * chunk, chunk)
        rows = pl.ds(r0, chunk)
        gcc = gc_s[rows, :]
        gct = gcc.T
        bc = beta_s[rows, :]
        cols = [slice(hd * A_HEAD_DIM, (hd + 1) * A_HEAD_DIM) for hd in heads]
        q = [q_s[rows, cs] for cs in cols]
        k = [k_s[rows, cs] for cs in cols]
        v = [v_s[rows, cs] for cs in cols]
        beta = [bc[:, hd:hd + 1] for hd in heads]
        gcol = [gcc[:, A_HEADS + hd:A_HEADS + hd + 1] for hd in heads]
        grow = [gct[A_HEADS + hd:A_HEADS + hd + 1, :] for hd in heads]
        glast = [gcc[chunk - 1:chunk, A_HEADS + hd:A_HEADS + hd + 1] for hd in heads]
        decay = [jnp.where(causal, jnp.exp(jnp.where(causal, gcol[hd] - grow[hd], 0.0)), 0.0) for hd in heads]
        eg = [jnp.exp(gcol[hd]) for hd in heads]
        kb = [k[hd] * beta[hd] for hd in heads]
        sc = [_dot_nt(jnp.concatenate([kb[hd], q[hd]], axis=0), k[hd]) for hd in heads]
        lmat = [jnp.where(strict, sc[hd][:chunk] * decay[hd], 0.0) for hd in heads]
        tmat = _unit_lower_inverses(lmat, chunk)
        uw = [_dot(tmat[hd], jnp.concatenate([v[hd] * beta[hd], kb[hd] * eg[hd]], axis=1)) for hd in heads]
        for hd in heads:
            j = c * A_HEADS + hd
            u_s[j] = uw[hd][:, :A_HEAD_DIM]
            wq_s[j] = jnp.concatenate([uw[hd][:, A_HEAD_DIM:], q[hd] * eg[hd]], axis=0).astype(BF16)
            attn_s[j] = (sc[hd][chunk:] * decay[hd]).astype(BF16)
            kg_s[j] = (k[hd] * jnp.exp(glast[hd] - gcol[hd])).astype(BF16)
            egl_s[j] = jnp.broadcast_to(jnp.exp(glast[hd]), (SUBLANES, LANES))
        return carry

    def scan_step(c, carry):
        r0 = pl.multiple_of(c * chunk, chunk)
        js = [c * A_HEADS + hd for hd in heads]
        s = [s_ref[hd] for hd in heads]
        ws = [_dot(wq_s[js[hd]], s[hd]) for hd in heads]
        v_new = [u_s[js[hd]] - ws[hd][:chunk] for hd in heads]
        o = [ws[hd][chunk:] + _dot(attn_s[js[hd]], v_new[hd]) for hd in heads]
        for hd in heads:
            s_ref[hd] = s[hd] * egl_s[js[hd]][0:1, :] + _dot_tn(kg_s[js[hd]], v_new[hd])
            o_s[pl.ds(r0, chunk), hd * A_HEAD_DIM:(hd + 1) * A_HEAD_DIM] = o[hd]
        return carry

    lax.fori_loop(0, tt // chunk, prep_step, 0)
    lax.fori_loop(0, tt // chunk, scan_step, 0)

    z = pa[:, 3 * A_WIDTH:4 * A_WIDTH]
    for hd in range(A_HEADS):
        lo = hd * A_HEAD_DIM
        o = o_s[:, lo:lo + A_HEAD_DIM]
        o = o * lax.rsqrt(jnp.mean(o * o, axis=-1, keepdims=True) + L2_EPS) * ng_ref[...]
        oa_ref[:, lo:lo + A_HEAD_DIM] = (o * _silu(z[:, lo:lo + A_HEAD_DIM])).astype(oa_ref.dtype)


def _gdn(x2d, norm1_g, w_a, conv_w, a_par, norm_g, *, batch, seq, tt, chunk):
    n, d = x2d.shape
    nt = seq // tt
    cols = w_a.shape[1]
    nj = tt // chunk * A_HEADS
    kern = functools.partial(_gdn_kernel, tt=tt, chunk=chunk)
    const = lambda b, t: (0, 0)
    return pl.pallas_call(
        kern,
        out_shape=jax.ShapeDtypeStruct((n, A_WIDTH), BF16),
        grid=(batch, nt),
        in_specs=[
            pl.BlockSpec((tt, d), lambda b, t: (b * nt + t, 0)),
            pl.BlockSpec((1, d), const),
            pl.BlockSpec((d, cols), const),
            pl.BlockSpec((A_CONV, 3 * A_WIDTH), const),
            pl.BlockSpec((SUBLANES, LANES), const),
            pl.BlockSpec((1, A_HEAD_DIM), const),
        ],
        out_specs=pl.BlockSpec((tt, A_WIDTH), lambda b, t: (b * nt + t, 0)),
        scratch_shapes=[
            pltpu.VMEM((tt + SUBLANES, 3 * A_WIDTH), F32),
            pltpu.VMEM((A_HEADS, A_HEAD_DIM, A_HEAD_DIM), F32),
            pltpu.VMEM((tt, A_WIDTH), F32),
            pltpu.VMEM((tt, A_WIDTH), F32),
            pltpu.VMEM((tt, A_WIDTH), F32),
            pltpu.VMEM((tt, A_WIDTH), F32),
            pltpu.VMEM((tt, LANES), F32),
            pltpu.VMEM((tt, LANES), F32),
            pltpu.VMEM((nj, chunk, A_HEAD_DIM), F32),
            pltpu.VMEM((nj, 2 * chunk, A_HEAD_DIM), BF16),
            pltpu.VMEM((nj, chunk, chunk), BF16),
            pltpu.VMEM((nj, chunk, A_HEAD_DIM), BF16),
            pltpu.VMEM((nj, SUBLANES, LANES), F32),
        ],
        compiler_params=pltpu.CompilerParams(
            dimension_semantics=("parallel", "arbitrary"), vmem_limit_bytes=VMEM_LIMIT),
        name="gdn",
    )(x2d, norm1_g, w_a, conv_w, a_par, norm_g)


def _rwkv_kernel(x_ref, g1_ref, wb_ref, mu_ref, wlora_ref, g2_ref, vec_ref, hsum_ref, ob_ref,
                 ext_ref, s_ref, ar_s, kb_s, hh_s, vs_s, av_s, arb_s, t_s, o_s, wtot_s, *, tt, chunk):
    bcols = 3 * B_WIDTH + B_DECAY_LORA + B_ICLR_LORA + B_GATE_LORA
    npairs = B_WIDTH // LANES

    @pl.when(pl.program_id(1) == 0)
    def _():
        ext_ref[0:SUBLANES, :] = jnp.zeros((SUBLANES, bcols), F32)
        s_ref[...] = jnp.zeros(s_ref.shape, F32)

    h = _rmsnorm(x_ref[...], g1_ref[...]).astype(BF16)
    pb = jnp.dot(h, wb_ref[...], preferred_element_type=F32)

    ext_ref[SUBLANES:SUBLANES + tt, :] = pb
    prev = ext_ref[SUBLANES - 1:SUBLANES - 1 + tt, :]
    ext_ref[0:SUBLANES, :] = ext_ref[tt:tt + SUBLANES, :]
    xm = pb + (prev - pb) * mu_ref[...]

    r = xm[:, 0:B_WIDTH]
    k = xm[:, B_WIDTH:2 * B_WIDTH]
    v = xm[:, 2 * B_WIDTH:3 * B_WIDTH]
    xwa = xm[:, 3 * B_WIDTH:3 * B_WIDTH + LANES]
    xg = xm[:, 3 * B_WIDTH + LANES:]
    lane = _iota2((tt, LANES), 1)
    lora_in = jnp.where(lane < B_DECAY_LORA, jnp.tanh(xwa), xwa)
    lora = _dot_x3(lora_in, wlora_ref[...])
    w0 = vec_ref[0:1, :]
    a0 = vec_ref[1:2, :]
    k_k = vec_ref[2:3, :]
    k_a = vec_ref[3:4, :]
    r_k = vec_ref[4:5, :]
    ln_w = vec_ref[5:6, :]
    ln_b = vec_ref[6:7, :]
    wl = -_softplus(-(w0 + lora[:, :B_WIDTH])) - 0.5
    lw = -jnp.exp(wl)
    a = _sigmoid(a0 + lora[:, B_WIDTH:])
    gate = _dot(_sigmoid(xg), g2_ref[...])

    hsum = hsum_ref[...]

    def head_sum(t):
        return jnp.dot(t.astype(BF16), hsum, preferred_element_type=F32)

    kk = k * k_k
    kk = kk * lax.rsqrt(head_sum(kk * kk) + L2_EPS)
    kf = k * (1.0 + (a - 1.0) * k_a)
    beta = kk * a

    lane2 = _iota2((chunk, LANES), 1)
    m0 = lane2 < B_HEAD_DIM

    def stack(*ts):
        parts = []
        for t in ts:
            parts += [jnp.where(m0, t, 0.0), jnp.where(m0, 0.0, t)]
        return jnp.concatenate(parts, axis=0).astype(BF16)

    tri = (_iota2((chunk, chunk), 1) <= _iota2((chunk, chunk), 0)).astype(F32)
    for c in range(tt // chunk):
        rs = slice(c * chunk, (c + 1) * chunk)
        lwc = lw[rs]
        gcum = _dot_exact_lhs(tri, lwc)
        gtot = gcum[chunk - 1:chunk, :]
        w_inv = jnp.exp(-gcum)
        w_rem = jnp.exp(gtot - gcum)
        at = kk[rs] * jnp.exp(gcum - lwc)
        kt = kf[rs] * w_inv
        bt = beta[rs] * w_inv
        rt = r[rs] * jnp.exp(gcum)
        kh = kf[rs] * w_rem
        bh = beta[rs] * w_rem
        vc = v[rs]
        wtot = jnp.broadcast_to(jnp.exp(gtot), (SUBLANES, B_WIDTH))
        for p in range(npairs):
            ls = slice(p * LANES, (p + 1) * LANES)
            j = c * npairs + p
            ar_s[j] = stack(at[:, ls], rt[:, ls])
            kb_s[j] = stack(kt[:, ls], bt[:, ls])
            hh_s[j] = stack(kh[:, ls], bh[:, ls])
            vs_s[j] = stack(vc[:, ls])
            wtot_s[j] = wtot[:, ls]

    n2 = 2 * chunk
    ri = _iota2((n2, n2), 0)
    ci = _iota2((n2, n2), 1)
    incl = ci <= ri
    strict = ci < ri
    pairs = range(npairs)

    def prep_step(c, carry):
        js = [c * npairs + p for p in pairs]
        sc = [_dot_nt(ar_s[j], kb_s[j]) for j in js]
        a_ab = [jnp.where(strict, s4[:n2, n2:], 0.0) for s4 in sc]
        tmat = _unit_lower_inverses(a_ab, chunk)
        for p in pairs:
            j = js[p]
            s4 = sc[p]
            a_kk = jnp.concatenate([jnp.where(strict, s4[:n2, :n2], 0.0),
                                    jnp.where(incl, s4[n2:, :n2], 0.0)], axis=0)
            av_s[j] = _dot(a_kk, vs_s[j])
            arb_s[j] = jnp.where(incl, s4[n2:, n2:], 0.0).astype(BF16)
            t_s[j] = tmat[p].astype(BF16)
        return carry

    def scan_step(c, carry):
        r0 = pl.multiple_of(c * chunk, chunk)
        js = [c * npairs + p for p in pairs]
        s = [s_ref[p] for p in pairs]
        ps = [_dot_nt(ar_s[js[p]], s[p]) + av_s[js[p]] for p in pairs]
        u = [_dot(t_s[js[p]], ps[p][:n2]) for p in pairs]
        o2 = [ps[p][n2:] - _dot(arb_s[js[p]], u[p]) for p in pairs]
        for p in pairs:
            vu = jnp.concatenate([vs_s[js[p]], (-u[p]).astype(BF16)], axis=0)
            s_ref[p] = s[p] * wtot_s[js[p]][0:1, :] + _dot_tn(vu, hh_s[js[p]])
            o_s[pl.ds(r0, chunk), p * LANES:(p + 1) * LANES] = o2[p][:chunk] + o2[p][chunk:]
        return carry

    lax.fori_loop(0, tt // chunk, prep_step, 0)
    lax.fori_loop(0, tt // chunk, scan_step, 0)

    out = o_s[...]
    mean = head_sum(out) * (1.0 / B_HEAD_DIM)
    cen = out - mean
    var = head_sum(cen * cen) * (1.0 / B_HEAD_DIM)
    out = cen * lax.rsqrt(var + B_GN_EPS) * ln_w + ln_b
    bonus = head_sum(r * kf * r_k) * v
    ob_ref[...] = ((out + bonus) * gate).astype(ob_ref.dtype)


def _rwkv(x2d, norm1_g, w_b, mu, w_lora, g2, vecs, hsum, *, batch, seq, tt, chunk):
    n, d = x2d.shape
    nt = seq // tt
    bcols = w_b.shape[1]
    kern = functools.partial(_rwkv_kernel, tt=tt, chunk=chunk)
    const = lambda b, t: (0, 0)
    nj = tt // chunk * (B_WIDTH // LANES)
    return pl.pallas_call(
        kern,
        out_shape=jax.ShapeDtypeStruct((n, B_WIDTH), BF16),
        grid=(batch, nt),
        in_specs=[
            pl.BlockSpec((tt, d), lambda b, t: (b * nt + t, 0)),
            pl.BlockSpec((1, d), const),
            pl.BlockSpec((d, bcols), const),
            pl.BlockSpec((1, bcols), const),
            pl.BlockSpec((LANES, 2 * B_WIDTH), const),
            pl.BlockSpec((B_GATE_LORA, B_WIDTH), const),
            pl.BlockSpec((SUBLANES, B_WIDTH), const),
            pl.BlockSpec((B_WIDTH, B_WIDTH), const),
        ],
        out_specs=pl.BlockSpec((tt, B_WIDTH), lambda b, t: (b * nt + t, 0)),
        scratch_shapes=[
            pltpu.VMEM((tt + SUBLANES, bcols), F32),
            pltpu.VMEM((B_WIDTH // LANES, LANES, LANES), F32),
            pltpu.VMEM((nj, 4 * chunk, LANES), BF16),
            pltpu.VMEM((nj, 4 * chunk, LANES), BF16),
            pltpu.VMEM((nj, 4 * chunk, LANES), BF16),
            pltpu.VMEM((nj, 2 * chunk, LANES), BF16),
            pltpu.VMEM((nj, 4 * chunk, LANES), F32),
            pltpu.VMEM((nj, 2 * chunk, 2 * chunk), BF16),
            pltpu.VMEM((nj, 2 * chunk, 2 * chunk), BF16),
            pltpu.VMEM((tt, B_WIDTH), F32),
            pltpu.VMEM((nj, SUBLANES, LANES), F32),
        ],
        compiler_params=pltpu.CompilerParams(
            dimension_semantics=("parallel", "arbitrary"), vmem_limit_bytes=VMEM_LIMIT),
        name="rwkv",
    )(x2d, norm1_g, w_b, mu, w_lora, g2, vecs, hsum)


def _merge_kernel(x_ref, g1_ref, oa_ref, ob_ref, wg_ref, wua_ref, wub_ref, wo_ref, x2_ref):
    d = x_ref.shape[1]
    x = x_ref[...]
    h = _rmsnorm(x, g1_ref[...]).astype(BF16)
    pg = jnp.dot(h, wg_ref[...], preferred_element_type=F32)
    ya = jnp.dot(oa_ref[...], wua_ref[...], preferred_element_type=F32)
    yb = jnp.dot(ob_ref[...], wub_ref[...], preferred_element_type=F32)
    merged = _sigmoid(pg[:, :d]) * ya + _sigmoid(pg[:, d:]) * yb
    x2_ref[...] = x + jnp.dot(merged.astype(BF16), wo_ref[...], preferred_element_type=F32)


def _merge(x2d, norm1_g, oa, ob, w_g, w_up_a, w_up_b, w_out, *, tm):
    n, d = x2d.shape
    const = lambda i: (0, 0)
    rows = lambda i: (i, 0)
    return pl.pallas_call(
        _merge_kernel,
        out_shape=jax.ShapeDtypeStruct((n, d), F32),
        grid=(n // tm,),
        in_specs=[
            pl.BlockSpec((tm, d), rows),
            pl.BlockSpec((1, d), const),
            pl.BlockSpec((tm, A_WIDTH), rows),
            pl.BlockSpec((tm, B_WIDTH), rows),
            pl.BlockSpec((d, 2 * d), const),
            pl.BlockSpec((A_WIDTH, d), const),
            pl.BlockSpec((B_WIDTH, d), const),
            pl.BlockSpec((d, d), const),
        ],
        out_specs=pl.BlockSpec((tm, d), rows),
        compiler_params=pltpu.CompilerParams(
            dimension_semantics=("parallel",), vmem_limit_bytes=VMEM_LIMIT),
        name="merge",
    )(x2d, norm1_g, oa, ob, w_g, w_up_a, w_up_b, w_out)


def _router_kernel(x2_ref, g2_ref, wr_ref, br_ref, h2_ref, meta_ref, cnt_ref, run_ref):
    tm = x2_ref.shape[0]

    @pl.when(pl.program_id(0) == 0)
    def _():
        run_ref[...] = jnp.zeros(run_ref.shape, F32)

    h2 = _rmsnorm(x2_ref[...], g2_ref[...])
    h2_ref[...] = h2
    logits = _dot_f32(h2, wr_ref[...]) + br_ref[...]
    lane = _iota2((tm, LANES), 1)
    neg = jnp.float32(-jnp.inf)

    work = logits
    sel_idx = []
    sel_val = []
    multi = jnp.zeros((tm, LANES), F32)
    for _ in range(TOP_K):
        mx = jnp.max(work, axis=-1, keepdims=True)
        idx = jnp.min(jnp.where(work == mx, lane, LANES), axis=-1, keepdims=True)
        hit = lane == idx
        multi = multi + hit.astype(F32)
        work = jnp.where(hit, neg, work)
        sel_idx.append(idx)
        sel_val.append(mx)
    ex = [jnp.exp(vv - sel_val[0]) for vv in sel_val]
    den = ex[0] + ex[1] + ex[2] + ex[3]
    gates = [e / den for e in ex]

    row = _iota2((tm, tm), 0)
    col = _iota2((tm, tm), 1)
    before = (col < row).astype(BF16)
    excl = jnp.dot(before, multi.astype(BF16), preferred_element_type=F32) + run_ref[0:1, :]
    run_ref[0:1, :] = run_ref[0:1, :] + jnp.sum(multi, axis=0, keepdims=True)
    cnt_ref[...] = run_ref[...]

    meta = jnp.zeros((tm, LANES), F32)
    for kk in range(TOP_K):
        rank = jnp.sum(jnp.where(lane == sel_idx[kk], excl, 0.0), axis=-1, keepdims=True)
        meta = jnp.where(lane == kk, sel_idx[kk].astype(F32), meta)
        meta = jnp.where(lane == TOP_K + kk, rank, meta)
        meta = jnp.where(lane == 2 * TOP_K + kk, gates[kk], meta)
    meta_ref[...] = meta


def _router(x2, norm2_g, w_r, b_r, *, tm):
    n, d = x2.shape
    const = lambda i: (0, 0)
    rows = lambda i: (i, 0)
    return pl.pallas_call(
        _router_kernel,
        out_shape=(jax.ShapeDtypeStruct((n, d), F32),
                   jax.ShapeDtypeStruct((n, LANES), F32),
                   jax.ShapeDtypeStruct((SUBLANES, LANES), F32)),
        grid=(n // tm,),
        in_specs=[
            pl.BlockSpec((tm, d), rows),
            pl.BlockSpec((1, d), const),
            pl.BlockSpec((d, LANES), const),
            pl.BlockSpec((1, LANES), const),
        ],
        out_specs=(pl.BlockSpec((tm, d), rows),
                   pl.BlockSpec((tm, LANES), rows),
                   pl.BlockSpec((SUBLANES, LANES), const)),
        scratch_shapes=[pltpu.VMEM((SUBLANES, LANES), F32)],
        compiler_params=pltpu.CompilerParams(
            dimension_semantics=("arbitrary",), vmem_limit_bytes=VMEM_LIMIT),
        name="router",
    )(x2, norm2_g, w_r, b_r)


def _dispatch_kernel(pos_ref, h2_ref, zero_ref, xs_ref, sem):
    del zero_ref
    tm = h2_ref.shape[0]
    base = pl.program_id(0) * tm * TOP_K

    def row_copy(t, kk):
        dst = pos_ref[base + t * TOP_K + kk]
        return pltpu.make_async_copy(h2_ref.at[pl.ds(t, 1), :], xs_ref.at[pl.ds(dst, 1), :], sem)

    def issue(t, carry):
        for kk in range(TOP_K):
            row_copy(t, kk).start()
        return carry

    def drain(t, carry):
        for kk in range(TOP_K):
            row_copy(t, kk).wait()
        return carry

    lax.fori_loop(0, tm, issue, 0)
    lax.fori_loop(0, tm, drain, 0)


def _dispatch(pos, h2, zeros, *, tm):
    n, d = h2.shape
    p_rows = zeros.shape[0]
    return pl.pallas_call(
        _dispatch_kernel,
        out_shape=jax.ShapeDtypeStruct((p_rows, d), F32),
        grid_spec=pltpu.PrefetchScalarGridSpec(
            num_scalar_prefetch=1,
            grid=(n // tm,),
            in_specs=[pl.BlockSpec((tm, d), lambda i, pos: (i, 0)),
                      pl.BlockSpec(memory_space=pl.ANY)],
            out_specs=pl.BlockSpec(memory_space=pl.ANY),
            scratch_shapes=[pltpu.SemaphoreType.DMA],
        ),
        input_output_aliases={2: 0},
        compiler_params=pltpu.CompilerParams(
            dimension_semantics=("arbitrary",), vmem_limit_bytes=VMEM_LIMIT, has_side_effects=True),
        name="dispatch",
    )(pos, h2, zeros)


def _experts_kernel(te_ref, nu_ref, xs_ref, w1_ref, b1_ref, w2_ref, b2_ref, y_ref):
    del te_ref
    de = w2_ref.shape[1]

    @pl.when(pl.program_id(0) < nu_ref[0])
    def _():
        x = xs_ref[...].astype(BF16)
        h1 = jnp.dot(x, w1_ref[0].astype(BF16), preferred_element_type=F32) + b1_ref[0]
        glu = jnp.minimum(h1[:, :de], SWIGLU_LIMIT)
        lin = jnp.clip(h1[:, de:], -SWIGLU_LIMIT, SWIGLU_LIMIT)
        act = glu * _sigmoid(SWIGLU_ALPHA * glu) * (lin + 1.0)
        y_ref[...] = jnp.dot(act.astype(BF16), w2_ref[0].astype(BF16), preferred_element_type=F32) + b2_ref[0]

    @pl.when(pl.program_id(0) >= nu_ref[0])
    def _():
        y_ref[...] = jnp.zeros(y_ref.shape, y_ref.dtype)


def _experts(tile_expert, n_used, xs, w1, b1, w2, b2, *, tm):
    p_rows, d = xs.shape
    ne, _, de2 = w1.shape
    de = w2.shape[1]
    n_tiles = p_rows // tm
    wsel = lambda i, te, nu: (te[i], 0, 0)
    rows = lambda i, te, nu: (i, 0)
    return pl.pallas_call(
        _experts_kernel,
        out_shape=jax.ShapeDtypeStruct((p_rows, d), F32),
        grid_spec=pltpu.PrefetchScalarGridSpec(
            num_scalar_prefetch=2,
            grid=(n_tiles,),
            in_specs=[pl.BlockSpec((tm, d), rows),
                      pl.BlockSpec((1, d, de2), wsel),
                      pl.BlockSpec((1, 1, de2), wsel),
                      pl.BlockSpec((1, de, d), wsel),
                      pl.BlockSpec((1, 1, d), wsel)],
            out_specs=pl.BlockSpec((tm, d), rows),
        ),
        compiler_params=pltpu.CompilerParams(
            dimension_semantics=("arbitrary",), vmem_limit_bytes=VMEM_LIMIT),
        name="experts",
    )(tile_expert, n_used, xs, w1, b1, w2, b2)


def _combine_kernel(pos_ref, x2_ref, meta_ref, fg_ref, y_ref, out_ref, buf, sem, *, final_norm):
    tm = x2_ref.shape[0]
    base = pl.program_id(0) * tm * TOP_K

    def row_copy(t, kk):
        src = pos_ref[base + t * TOP_K + kk]
        return pltpu.make_async_copy(y_ref.at[pl.ds(src, 1), :], buf.at[kk, pl.ds(t, 1), :], sem)

    def issue(t, carry):
        for kk in range(TOP_K):
            row_copy(t, kk).start()
        return carry

    def drain(t, carry):
        for kk in range(TOP_K):
            row_copy(t, kk).wait()
        return carry

    lax.fori_loop(0, tm, issue, 0)
    lax.fori_loop(0, tm, drain, 0)

    acc = x2_ref[...]
    meta = meta_ref[...]
    for kk in range(TOP_K):
        acc = acc + buf[kk] * meta[:, 2 * TOP_K + kk:2 * TOP_K + kk + 1]
    out_ref[...] = _rmsnorm(acc, fg_ref[...]) if final_norm else acc


def _combine(pos, x2, meta, final_g, y, *, tm, final_norm):
    n, d = x2.shape
    return pl.pallas_call(
        functools.partial(_combine_kernel, final_norm=final_norm),
        out_shape=jax.ShapeDtypeStruct((n, d), F32),
        grid_spec=pltpu.PrefetchScalarGridSpec(
            num_scalar_prefetch=1,
            grid=(n // tm,),
            in_specs=[pl.BlockSpec((tm, d), lambda i, pos: (i, 0)),
                      pl.BlockSpec((tm, LANES), lambda i, pos: (i, 0)),
                      pl.BlockSpec((1, d), lambda i, pos: (0, 0)),
                      pl.BlockSpec(memory_space=pl.ANY)],
            out_specs=pl.BlockSpec((tm, d), lambda i, pos: (i, 0)),
            scratch_shapes=[pltpu.VMEM((TOP_K, tm, d), F32), pltpu.SemaphoreType.DMA],
        ),
        compiler_params=pltpu.CompilerParams(
            dimension_semantics=("arbitrary",), vmem_limit_bytes=VMEM_LIMIT),
        name="combine",
    )(pos, x2, meta, final_g, y)


def _pad_cols(w, cols):
    return jnp.pad(w, ((0, 0), (0, cols - w.shape[1])))


def _layer(x2d, p, *, batch, seq, tiles):
    d = x2d.shape[1]
    a_cols = 4 * A_WIDTH + 2 * A_HEADS
    b_cols = 3 * B_WIDTH + B_DECAY_LORA + B_ICLR_LORA + B_GATE_LORA
    w_in = p["w_in"]
    w_a = _pad_cols(w_in[:, :a_cols], 4 * A_WIDTH + LANES).astype(BF16)
    w_b = w_in[:, a_cols:a_cols + b_cols].astype(BF16)
    w_g = w_in[:, a_cols + b_cols:].astype(BF16)
    g1 = p["norm1_g"].reshape(1, d)

    a_par = jnp.zeros((SUBLANES, LANES), F32)
    a_par = a_par.at[0, A_HEADS:2 * A_HEADS].set(-jnp.exp(p["a_A_log"]))
    a_par = a_par.at[1, A_HEADS:2 * A_HEADS].set(p["a_dt_bias"])
    oa = _gdn(x2d, g1, w_a, p["a_conv_w"], a_par, p["a_norm_g"].reshape(1, A_HEAD_DIM),
              batch=batch, seq=seq, tt=tiles["tt_a"], chunk=A_CHUNK)

    w_lora = jnp.zeros((LANES, 2 * B_WIDTH), F32)
    w_lora = w_lora.at[:B_DECAY_LORA, :B_WIDTH].set(p["b_w2"])
    w_lora = w_lora.at[B_DECAY_LORA:, B_WIDTH:].set(p["b_a2"])
    vecs = jnp.stack([p["b_w0"], p["b_a0"], p["b_k_k"], p["b_k_a"], p["b_r_k"].reshape(-1),
                      p["b_ln_w"], p["b_ln_b"], jnp.zeros((B_WIDTH,), F32)])
    hd = jnp.arange(B_WIDTH) // B_HEAD_DIM
    hsum = (hd[:, None] == hd[None, :]).astype(BF16)
    ob = _rwkv(x2d, g1, w_b, p["b_mu"].reshape(1, b_cols), w_lora, p["b_g2"], vecs, hsum,
               batch=batch, seq=seq, tt=tiles["tt_b"], chunk=B_CHUNK)

    x2 = _merge(x2d, g1, oa, ob, w_g, p["w_up_a"].astype(BF16), p["w_up_b"].astype(BF16),
                p["w_out"].astype(BF16), tm=tiles["tm_merge"])

    w_r = _pad_cols(p["w_router"], LANES)
    b_r = jnp.full((1, LANES), -jnp.inf, F32).at[0, :N_EXPERTS].set(p["b_router"])
    h2, meta, cnt = _router(x2, p["norm2_g"].reshape(1, d), w_r, b_r, tm=tiles["tm_router"])

    tm_e = tiles["tm_expert"]
    n = x2d.shape[0]
    counts = cnt[0, :N_EXPERTS].astype(I32)
    tiles_per = (counts + tm_e - 1) // tm_e
    tile_end = jnp.cumsum(tiles_per)
    pad_off = (tile_end - tiles_per) * tm_e
    n_tiles = (n * TOP_K) // tm_e + N_EXPERTS
    n_used = tile_end[-1:]
    tile_ids = jnp.arange(n_tiles, dtype=I32)
    tile_expert = jnp.minimum(jnp.sum(tile_ids[:, None] >= tile_end[None, :], axis=1), N_EXPERTS - 1).astype(I32)
    last_e = tile_expert[jnp.maximum(n_used[0] - 1, 0)]
    tile_expert = jnp.where(tile_ids < n_used[0], tile_expert, last_e)
    idx = meta[:, :TOP_K].astype(I32)
    rank = meta[:, TOP_K:2 * TOP_K].astype(I32)
    pos = (pad_off[idx] + rank).reshape(-1)

    xs = _dispatch(pos, h2, jnp.zeros((n_tiles * tm_e, d), F32), tm=tiles["tm_dispatch"])
    y = _experts(tile_expert, n_used.astype(I32), xs, p["w1"], p["b1"].reshape(N_EXPERTS, 1, -1),
                 p["w2"], p["b2"].reshape(N_EXPERTS, 1, -1), tm=tm_e)
    return pos, x2, meta, y


def _tiles(seq, n):
    return dict(tt_a=min(seq, 512), tt_b=min(seq, 512), tm_merge=min(n, 512), tm_router=min(n, 512),
                tm_expert=min(n, 512), tm_dispatch=min(n, 512), tm_combine=min(n, 256))


def kernel(x, norm1_g, w_in, a_conv_w, a_A_log, a_dt_bias, a_norm_g, b_mu, b_w0, b_w2, b_a0, b_a2, b_g2, b_k_k, b_k_a, b_r_k, b_ln_w, b_ln_b, w_up_a, w_up_b, w_out, norm2_g, w_router, b_router, w1, b1, w2, b2, final_g):
    batch, seq, d = x.shape
    n = batch * seq
    depth = norm1_g.shape[0]
    tiles = _tiles(seq, n)
    params = dict(norm1_g=norm1_g, w_in=w_in, a_conv_w=a_conv_w, a_A_log=a_A_log, a_dt_bias=a_dt_bias,
                  a_norm_g=a_norm_g, b_mu=b_mu, b_w0=b_w0, b_w2=b_w2, b_a0=b_a0, b_a2=b_a2, b_g2=b_g2,
                  b_k_k=b_k_k, b_k_a=b_k_a, b_r_k=b_r_k, b_ln_w=b_ln_w, b_ln_b=b_ln_b, w_up_a=w_up_a,
                  w_up_b=w_up_b, w_out=w_out, norm2_g=norm2_g, w_router=w_router, b_router=b_router,
                  w1=w1, b1=b1, w2=w2, b2=b2)
    x2d = x.reshape(n, d)
    for l in range(depth):
        p = {k: v[l] for k, v in params.items()}
        pos, x2, meta, y = _layer(x2d, p, batch=batch, seq=seq, tiles=tiles)
        x2d = _combine(pos, x2, meta, final_g.reshape(1, d), y, tm=tiles["tm_combine"],
                       final_norm=(l == depth - 1))
    return x2d.reshape(batch, seq, d)
```

```python
import functools

import jax
import jax.numpy as jnp
from jax import lax
from jax.experimental import pallas as pl
from jax.experimental.pallas import tpu as pltpu

F32 = jnp.float32
BF16 = jnp.bfloat16
I32 = jnp.int32

A_HEADS = 4
A_HEAD_DIM = 128
A_WIDTH = A_HEADS * A_HEAD_DIM
A_CONV = 4
B_HEADS = 8
B_HEAD_DIM = 64
B_WIDTH = B_HEADS * B_HEAD_DIM
B_DECAY_LORA = 64
B_ICLR_LORA = 64
B_GATE_LORA = 128
B_GN_EPS = 64e-5
N_EXPERTS = 32
TOP_K = 4
SWIGLU_ALPHA = 1.702
SWIGLU_LIMIT = 7.0
RMS_EPS = 1e-5
L2_EPS = 1e-6

LANES = 128
SUBLANES = 8
INV_BLOCK = 16
A_CHUNK = 128
B_CHUNK = 64
VMEM_LIMIT = 56 * 1024 * 1024


def _dot(a, b):
    return jnp.dot(a.astype(BF16), b.astype(BF16), preferred_element_type=F32)


def _dot_nt(a, b):
    return lax.dot_general(a.astype(BF16), b.astype(BF16), (((1,), (1,)), ((), ())),
                           preferred_element_type=F32)


def _dot_tn(a, b):
    return lax.dot_general(a.astype(BF16), b.astype(BF16), (((0,), (0,)), ((), ())),
                           preferred_element_type=F32)


def _dot_f32(a, b):
    return jnp.dot(a, b, preferred_element_type=F32, precision=lax.Precision.HIGHEST)


def _split_bf16(a, terms):
    out = []
    for _ in range(terms):
        hi = a.astype(BF16)
        out.append(hi)
        a = a - hi.astype(F32)
    return out


def _dot_x3(a, b):
    ah, al = _split_bf16(a, 2)
    bh, bl = _split_bf16(b, 2)
    mm = lambda u, v: jnp.dot(u, v, preferred_element_type=F32)
    return mm(ah, bh) + mm(ah, bl) + mm(al, bh)


def _dot_exact_lhs(a01, b):
    a01 = a01.astype(BF16)
    return sum(jnp.dot(a01, t, preferred_element_type=F32) for t in _split_bf16(b, 3))


def _sigmoid(x):
    return 1.0 / (1.0 + jnp.exp(-x))


def _silu(x):
    return x * _sigmoid(x)


def _softplus(x):
    return jnp.maximum(x, 0.0) + jnp.log(1.0 + jnp.exp(-jnp.abs(x)))


def _rmsnorm(x, g):
    return x * lax.rsqrt(jnp.mean(x * x, axis=-1, keepdims=True) + RMS_EPS) * g


def _iota2(shape, dim):
    return lax.broadcasted_iota(I32, shape, dim)


def _unit_lower_inverses(mats, unit):
    n = mats[0].shape[0]
    row = _iota2((n, n), 0)
    col = _iota2((n, n), 1)
    eye = (row == col).astype(F32)
    same_blk = (row // INV_BLOCK) == (col // INV_BLOCK)
    ps = [jnp.where(same_blk, -a, 0.0) for a in mats]
    dinv = [eye + x for x in ps]
    for _ in range(3):
        ps = [_dot(p, p) for p in ps]
        dinv = [dv + _dot(dv, p) for dv, p in zip(dinv, ps)]
    ps = [-_dot(dv, jnp.where(same_blk, 0.0, a)) for dv, a in zip(dinv, mats)]
    ts = [eye + y for y in ps]
    k = 2
    while k < unit // INV_BLOCK:
        ps = [_dot(p, p) for p in ps]
        ts = [t + _dot(t, p) for t, p in zip(ts, ps)]
        k *= 2
    return [_dot(t, dv) for t, dv in zip(ts, dinv)]


def _gdn_kernel(x_ref, g1_ref, wa_ref, convw_ref, apar_ref, ng_ref, oa_ref,
                ext_ref, s_ref, q_s, k_s, v_s, o_s, beta_s, gc_s, u_s, wq_s, attn_s, kg_s, egl_s, *, tt, chunk):
    qkv_w = 3 * A_WIDTH

    @pl.when(pl.program_id(1) == 0)
    def _():
        ext_ref[0:SUBLANES, :] = jnp.zeros((SUBLANES, qkv_w), F32)
        s_ref[...] = jnp.zeros(s_ref.shape, F32)

    h = _rmsnorm(x_ref[...], g1_ref[...]).astype(BF16)
    pa = jnp.dot(h, wa_ref[...], preferred_element_type=F32)

    ext_ref[SUBLANES:SUBLANES + tt, :] = pa[:, :qkv_w]
    conv = jnp.zeros((tt, qkv_w), F32)
    for j in range(A_CONV):
        off = SUBLANES - (A_CONV - 1) + j
        conv = conv + convw_ref[j:j + 1, :] * ext_ref[off:off + tt, :]
    ext_ref[0:SUBLANES, :] = ext_ref[tt:tt + SUBLANES, :]
    qkv = _silu(conv)

    for hd in range(A_HEADS):
        lo = hd * A_HEAD_DIM
        qh = qkv[:, lo:lo + A_HEAD_DIM]
        kh = qkv[:, A_WIDTH + lo:A_WIDTH + lo + A_HEAD_DIM]
        qn = qh * lax.rsqrt(jnp.sum(qh * qh, axis=-1, keepdims=True) + L2_EPS) * (A_HEAD_DIM ** -0.5)
        kn = kh * lax.rsqrt(jnp.sum(kh * kh, axis=-1, keepdims=True) + L2_EPS)
        q_s[:, lo:lo + A_HEAD_DIM] = qn
        k_s[:, lo:lo + A_HEAD_DIM] = kn
    v_s[...] = qkv[:, 2 * A_WIDTH:]

    bg = pa[:, 4 * A_WIDTH:]
    beta_s[...] = _sigmoid(bg)
    g_all = apar_ref[0:1, :] * _softplus(bg + apar_ref[1:2, :])
    ri = _iota2((chunk, chunk), 0)
    ci = _iota2((chunk, chunk), 1)
    causal = ci <= ri
    strict = ci < ri
    tri = causal.astype(F32)
    for c in range(tt // chunk):
        gc_s[c * chunk:(c + 1) * chunk, :] = _dot_exact_lhs(tri, g_all[c * chunk:(c + 1) * chunk, :])

    heads = range(A_HEADS)

    def prep_step(c, carry):
        r0 = pl.multiple_of(c * chunk, chunk)
        rows = pl.ds(r0, chunk)
        gcc = gc_s[rows, :]
        gct = gcc.T
        bc = beta_s[rows, :]
        cols = [slice(hd * A_HEAD_DIM, (hd + 1) * A_HEAD_DIM) for hd in heads]
        q = [q_s[rows, cs] for cs in cols]
        k = [k_s[rows, cs] for cs in cols]
        v = [v_s[rows, cs] for cs in cols]
        beta = [bc[:, hd:hd + 1] for hd in heads]
        gcol = [gcc[:, A_HEADS + hd:A_HEADS + hd + 1] for hd in heads]
        grow = [gct[A_HEADS + hd:A_HEADS + hd + 1, :] for hd in heads]
        glast = [gcc[chunk - 1:chunk, A_HEADS + hd:A_HEADS + hd + 1] for hd in heads]
        decay = [jnp.where(causal, jnp.exp(jnp.where(causal, gcol[hd] - grow[hd], 0.0)), 0.0) for hd in heads]
        eg = [jnp.exp(gcol[hd]) for hd in heads]
        kb = [k[hd] * beta[hd] for hd in heads]
        sc = [_dot_nt(jnp.concatenate([kb[hd], q[hd]], axis=0), k[hd]) for hd in heads]
        lmat = [jnp.where(strict, sc[hd][:chunk] * decay[hd], 0.0) for hd in heads]
        tmat = _unit_lower_inverses(lmat, chunk)
        uw = [_dot(tmat[hd], jnp.concatenate([v[hd] * beta[hd], kb[hd] * eg[hd]], axis=1)) for hd in heads]
        for hd in heads:
            j = c * A_HEADS + hd
            u_s[j] = uw[hd][:, :A_HEAD_DIM]
            wq_s[j] = jnp.concatenate([uw[hd][:, A_HEAD_DIM:], q[hd] * eg[hd]], axis=0).astype(BF16)
            attn_s[j] = (sc[hd][chunk:] * decay[hd]).astype(BF16)
            kg_s[j] = (k[hd] * jnp.exp(glast[hd] - gcol[hd])).astype(BF16)
            egl_s[j] = jnp.broadcast_to(jnp.exp(glast[hd]), (SUBLANES, LANES))
        return carry

    def scan_step(c, carry):
        r0 = pl.multiple_of(c * chunk, chunk)
        js = [c * A_HEADS + hd for hd in heads]
        s = [s_ref[hd] for hd in heads]
        ws = [_dot(wq_s[js[hd]], s[hd]) for hd in heads]
        v_new = [u_s[js[hd]] - ws[hd][:chunk] for hd in heads]
        o = [ws[hd][chunk:] + _dot(attn_s[js[hd]], v_new[hd]) for hd in heads]
        for hd in heads:
            s_ref[hd] = s[hd] * egl_s[js[hd]][0:1, :] + _dot_tn(kg_s[js[hd]], v_new[hd])
            o_s[pl.ds(r0, chunk), hd * A_HEAD_DIM:(hd + 1) * A_HEAD_DIM] = o[hd]
        return carry

    lax.fori_loop(0, tt // chunk, prep_step, 0)
    lax.fori_loop(0, tt // chunk, scan_step, 0)

    z = pa[:, 3 * A_WIDTH:4 * A_WIDTH]
    for hd in range(A_HEADS):
        lo = hd * A_HEAD_DIM
        o = o_s[:, lo:lo + A_HEAD_DIM]
        o = o * lax.rsqrt(jnp.mean(o * o, axis=-1, keepdims=True) + L2_EPS) * ng_ref[...]
        oa_ref[:, lo:lo + A_HEAD_DIM] = (o * _silu(z[:, lo:lo + A_HEAD_DIM])).astype(oa_ref.dtype)


def _gdn(x2d, norm1_g, w_a, conv_w, a_par, norm_g, *, batch, seq, tt, chunk):
    n, d = x2d.shape
    nt = seq // tt
    cols = w_a.shape[1]
    nj = tt // chunk * A_HEADS
    kern = functools.partial(_gdn_kernel, tt=tt, chunk=chunk)
    const = lambda b, t: (0, 0)
    return pl.pallas_call(
        kern,
        out_shape=jax.ShapeDtypeStruct((n, A_WIDTH), BF16),
        grid=(batch, nt),
        in_specs=[
            pl.BlockSpec((tt, d), lambda b, t: (b * nt + t, 0)),
            pl.BlockSpec((1, d), const),
            pl.BlockSpec((d, cols), const),
            pl.BlockSpec((A_CONV, 3 * A_WIDTH), const),
            pl.BlockSpec((SUBLANES, LANES), const),
            pl.BlockSpec((1, A_HEAD_DIM), const),
        ],
        out_specs=pl.BlockSpec((tt, A_WIDTH), lambda b, t: (b * nt + t, 0)),
        scratch_shapes=[
            pltpu.VMEM((tt + SUBLANES, 3 * A_WIDTH), F32),
            pltpu.VMEM((A_HEADS, A_HEAD_DIM, A_HEAD_DIM), F32),
            pltpu.VMEM((tt, A_WIDTH), F32),
            pltpu.VMEM((tt, A_WIDTH), F32),
            pltpu.VMEM((tt, A_WIDTH), F32),
            pltpu.VMEM((tt, A_WIDTH), F32),
            pltpu.VMEM((tt, LANES), F32),
            pltpu.VMEM((tt, LANES), F32),
            pltpu.VMEM((nj, chunk, A_HEAD_DIM), F32),
            pltpu.VMEM((nj, 2 * chunk, A_HEAD_DIM), BF16),
            pltpu.VMEM((nj, chunk, chunk), BF16),
            pltpu.VMEM((nj, chunk, A_HEAD_DIM), BF16),
            pltpu.VMEM((nj, SUBLANES, LANES), F32),
        ],
        compiler_params=pltpu.CompilerParams(
            dimension_semantics=("parallel", "arbitrary"), vmem_limit_bytes=VMEM_LIMIT),
        name="gdn",
    )(x2d, norm1_g, w_a, conv_w, a_par, norm_g)


def _rwkv_kernel(x_ref, g1_ref, wb_ref, mu_ref, wlora_ref, g2_ref, vec_ref, hsum_ref, ob_ref,
                 ext_ref, s_ref, ar_s, kb_s, hh_s, vs_s, av_s, arb_s, t_s, o_s, wtot_s, *, tt, chunk):
    bcols = 3 * B_WIDTH + B_DECAY_LORA + B_ICLR_LORA + B_GATE_LORA
    npairs = B_WIDTH // LANES

    @pl.when(pl.program_id(1) == 0)
    def _():
        ext_ref[0:SUBLANES, :] = jnp.zeros((SUBLANES, bcols), F32)
        s_ref[...] = jnp.zeros(s_ref.shape, F32)

    h = _rmsnorm(x_ref[...], g1_ref[...]).astype(BF16)
    pb = jnp.dot(h, wb_ref[...], preferred_element_type=F32)

    ext_ref[SUBLANES:SUBLANES + tt, :] = pb
    prev = ext_ref[SUBLANES - 1:SUBLANES - 1 + tt, :]
    ext_ref[0:SUBLANES, :] = ext_ref[tt:tt + SUBLANES, :]
    xm = pb + (prev - pb) * mu_ref[...]

    r = xm[:, 0:B_WIDTH]
    k = xm[:, B_WIDTH:2 * B_WIDTH]
    v = xm[:, 2 * B_WIDTH:3 * B_WIDTH]
    xwa = xm[:, 3 * B_WIDTH:3 * B_WIDTH + LANES]
    xg = xm[:, 3 * B_WIDTH + LANES:]
    lane = _iota2((tt, LANES), 1)
    lora_in = jnp.where(lane < B_DECAY_LORA, jnp.tanh(xwa), xwa)
    lora = _dot_x3(lora_in, wlora_ref[...])
    w0 = vec_ref[0:1, :]
    a0 = vec_ref[1:2, :]
    k_k = vec_ref[2:3, :]
    k_a = vec_ref[3:4, :]
    r_k = vec_ref[4:5, :]
    ln_w = vec_ref[5:6, :]
    ln_b = vec_ref[6:7, :]
    wl = -_softplus(-(w0 + lora[:, :B_WIDTH])) - 0.5
    lw = -jnp.exp(wl)
    a = _sigmoid(a0 + lora[:, B_WIDTH:])
    gate = _dot(_sigmoid(xg), g2_ref[...])

    hsum = hsum_ref[...]

    def head_sum(t):
        return jnp.dot(t.astype(BF16), hsum, preferred_element_type=F32)

    kk = k * k_k
    kk = kk * lax.rsqrt(head_sum(kk * kk) + L2_EPS)
    kf = k * (1.0 + (a - 1.0) * k_a)
    beta = kk * a

    lane2 = _iota2((chunk, LANES), 1)
    m0 = lane2 < B_HEAD_DIM

    def stack(*ts):
        parts = []
        for t in ts:
            parts += [jnp.where(m0, t, 0.0), jnp.where(m0, 0.0, t)]
        return jnp.concatenate(parts, axis=0).astype(BF16)

    tri = (_iota2((chunk, chunk), 1) <= _iota2((chunk, chunk), 0)).astype(F32)
    for c in range(tt // chunk):
        rs = slice(c * chunk, (c + 1) * chunk)
        lwc = lw[rs]
        gcum = _dot_exact_lhs(tri, lwc)
        gtot = gcum[chunk - 1:chunk, :]
        w_inv = jnp.exp(-gcum)
        w_rem = jnp.exp(gtot - gcum)
        at = kk[rs] * jnp.exp(gcum - lwc)
        kt = kf[rs] * w_inv
        bt = beta[rs] * w_inv
        rt = r[rs] * jnp.exp(gcum)
        kh = kf[rs] * w_rem
        bh = beta[rs] * w_rem
        vc = v[rs]
        wtot = jnp.broadcast_to(jnp.exp(gtot), (SUBLANES, B_WIDTH))
        for p in range(npairs):
            ls = slice(p * LANES, (p + 1) * LANES)
            j = c * npairs + p
            ar_s[j] = stack(at[:, ls], rt[:, ls])
            kb_s[j] = stack(kt[:, ls], bt[:, ls])
            hh_s[j] = stack(kh[:, ls], bh[:, ls])
            vs_s[j] = stack(vc[:, ls])
            wtot_s[j] = wtot[:, ls]

    n2 = 2 * chunk
    ri = _iota2((n2, n2), 0)
    ci = _iota2((n2, n2), 1)
    incl = ci <= ri
    strict = ci < ri
    pairs = range(npairs)

    def prep_step(c, carry):
        js = [c * npairs + p for p in pairs]
        sc = [_dot_nt(ar_s[j], kb_s[j]) for j in js]
        a_ab = [jnp.where(strict, s4[:n2, n2:], 0.0) for s4 in sc]
        tmat = _unit_lower_inverses(a_ab, chunk)
        for p in pairs:
            j = js[p]
            s4 = sc[p]
            a_kk = jnp.concatenate([jnp.where(strict, s4[:n2, :n2], 0.0),
                                    jnp.where(incl, s4[n2:, :n2], 0.0)], axis=0)
            av_s[j] = _dot(a_kk, vs_s[j])
            arb_s[j] = jnp.where(incl, s4[n2:, n2:], 0.0).astype(BF16)
            t_s[j] = tmat[p].astype(BF16)
        return carry

    def scan_step(c, carry):
        r0 = pl.multiple_of(c * chunk, chunk)
        js = [c * npairs + p for p in pairs]
        s = [s_ref[p] for p in pairs]
        ps = [_dot_nt(ar_s[js[p]], s[p]) + av_s[js[p]] for p in pairs]
        u = [_dot(t_s[js[p]], ps[p][:n2]) for p in pairs]
        o2 = [ps[p][n2:] - _dot(arb_s[js[p]], u[p]) for p in pairs]
        for p in pairs:
            vu = jnp.concatenate([vs_s[js[p]], (-u[p]).astype(BF16)], axis=0)
            s_ref[p] = s[p] * wtot_s[js[p]][0:1, :] + _dot_tn(vu, hh_s[js[p]])
            o_s[pl.ds(r0, chunk), p * LANES:(p + 1) * LANES] = o2[p][:chunk] + o2[p][chunk:]
        return carry

    lax.fori_loop(0, tt // chunk, prep_step, 0)
    lax.fori_loop(0, tt // chunk, scan_step, 0)

    out = o_s[...]
    mean = head_sum(out) * (1.0 / B_HEAD_DIM)
    cen = out - mean
    var = head_sum(cen * cen) * (1.0 / B_HEAD_DIM)
    out = cen * lax.rsqrt(var + B_GN_EPS) * ln_w + ln_b
    bonus = head_sum(r * kf * r_k) * v
    ob_ref[...] = ((out + bonus) * gate).astype(ob_ref.dtype)


def _rwkv(x2d, norm1_g, w_b, mu, w_lora, g2, vecs, hsum, *, batch, seq, tt, chunk):
    n, d = x2d.shape
    nt = seq // tt
    bcols = w_b.shape[1]
    kern = functools.partial(_rwkv_kernel, tt=tt, chunk=chunk)
    const = lambda b, t: (0, 0)
    nj = tt // chunk * (B_WIDTH // LANES)
    return pl.pallas_call(
        kern,
        out_shape=jax.ShapeDtypeStruct((n, B_WIDTH), BF16),
        grid=(batch, nt),
        in_specs=[
            pl.BlockSpec((tt, d), lambda b, t: (b * nt + t, 0)),
            pl.BlockSpec((1, d), const),
            pl.BlockSpec((d, bcols), const),
            pl.BlockSpec((1, bcols), const),
            pl.BlockSpec((LANES, 2 * B_WIDTH), const),
            pl.BlockSpec((B_GATE_LORA, B_WIDTH), const),
            pl.BlockSpec((SUBLANES, B_WIDTH), const),
            pl.BlockSpec((B_WIDTH, B_WIDTH), const),
        ],
        out_specs=pl.BlockSpec((tt, B_WIDTH), lambda b, t: (b * nt + t, 0)),
        scratch_shapes=[
            pltpu.VMEM((tt + SUBLANES, bcols), F32),
            pltpu.VMEM((B_WIDTH // LANES, LANES, LANES), F32),
            pltpu.VMEM((nj, 4 * chunk, LANES), BF16),
            pltpu.VMEM((nj, 4 * chunk, LANES), BF16),
            pltpu.VMEM((nj, 4 * chunk, LANES), BF16),
            pltpu.VMEM((nj, 2 * chunk, LANES), BF16),
            pltpu.VMEM((nj, 4 * chunk, LANES), F32),
            pltpu.VMEM((nj, 2 * chunk, 2 * chunk), BF16),
            pltpu.VMEM((nj, 2 * chunk, 2 * chunk), BF16),
            pltpu.VMEM((tt, B_WIDTH), F32),
            pltpu.VMEM((nj, SUBLANES, LANES), F32),
        ],
        compiler_params=pltpu.CompilerParams(
            dimension_semantics=("parallel", "arbitrary"), vmem_limit_bytes=VMEM_LIMIT),
        name="rwkv",
    )(x2d, norm1_g, w_b, mu, w_lora, g2, vecs, hsum)


def _merge_kernel(x_ref, g1_ref, oa_ref, ob_ref, wg_ref, wua_ref, wub_ref, wo_ref, x2_ref):
    d = x_ref.shape[1]
    x = x_ref[...]
    h = _rmsnorm(x, g1_ref[...]).astype(BF16)
    pg = jnp.dot(h, wg_ref[...], preferred_element_type=F32)
    ya = jnp.dot(oa_ref[...], wua_ref[...], preferred_element_type=F32)
    yb = jnp.dot(ob_ref[...], wub_ref[...], preferred_element_type=F32)
    merged = _sigmoid(pg[:, :d]) * ya + _sigmoid(pg[:, d:]) * yb
    x2_ref[...] = x + jnp.dot(merged.astype(BF16), wo_ref[...], preferred_element_type=F32)


def _merge(x2d, norm1_g, oa, ob, w_g, w_up_a, w_up_b, w_out, *, tm):
    n, d = x2d.shape
    const = lambda i: (0, 0)
    rows = lambda i: (i, 0)
    return pl.pallas_call(
        _merge_kernel,
        out_shape=jax.ShapeDtypeStruct((n, d), F32),
        grid=(n // tm,),
        in_specs=[
            pl.BlockSpec((tm, d), rows),
            pl.BlockSpec((1, d), const),
            pl.BlockSpec((tm, A_WIDTH), rows),
            pl.BlockSpec((tm, B_WIDTH), rows),
            pl.BlockSpec((d, 2 * d), const),
            pl.BlockSpec((A_WIDTH, d), const),
            pl.BlockSpec((B_WIDTH, d), const),
            pl.BlockSpec((d, d), const),
        ],
        out_specs=pl.BlockSpec((tm, d), rows),
        compiler_params=pltpu.CompilerParams(
            dimension_semantics=("parallel",), vmem_limit_bytes=VMEM_LIMIT),
        name="merge",
    )(x2d, norm1_g, oa, ob, w_g, w_up_a, w_up_b, w_out)


def _router_kernel(x2_ref, g2_ref, wr_ref, br_ref, h2_ref, meta_ref, cnt_ref, run_ref):
    tm = x2_ref.shape[0]

    @pl.when(pl.program_id(0) == 0)
    def _():
        run_ref[...] = jnp.zeros(run_ref.shape, F32)

    h2 = _rmsnorm(x2_ref[...], g2_ref[...])
    h2_ref[...] = h2
    logits = _dot_f32(h2, wr_ref[...]) + br_ref[...]
    lane = _iota2((tm, LANES), 1)
    neg = jnp.float32(-jnp.inf)

    work = logits
    sel_idx = []
    sel_val = []
    multi = jnp.zeros((tm, LANES), F32)
    for _ in range(TOP_K):
        mx = jnp.max(work, axis=-1, keepdims=True)
        idx = jnp.min(jnp.where(work == mx, lane, LANES), axis=-1, keepdims=True)
        hit = lane == idx
        multi = multi + hit.astype(F32)
        work = jnp.where(hit, neg, work)
        sel_idx.append(idx)
        sel_val.append(mx)
    ex = [jnp.exp(vv - sel_val[0]) for vv in sel_val]
    den = ex[0] + ex[1] + ex[2] + ex[3]
    gates = [e / den for e in ex]

    run_ref[0:1, :] = run_ref[0:1, :] + jnp.sum(multi, axis=0, keepdims=True)
    cnt_ref[...] = run_ref[...]

    meta = jnp.zeros((tm, LANES), F32)
    for kk in range(TOP_K):
        meta = jnp.where(lane == kk, sel_idx[kk].astype(F32), meta)
        meta = jnp.where(lane == TOP_K + kk, gates[kk], meta)
    meta_ref[...] = meta


def _router(x2, norm2_g, w_r, b_r, *, tm):
    n, d = x2.shape
    const = lambda i: (0, 0)
    rows = lambda i: (i, 0)
    return pl.pallas_call(
        _router_kernel,
        out_shape=(jax.ShapeDtypeStruct((n, d), F32),
                   jax.ShapeDtypeStruct((n, LANES), F32),
                   jax.ShapeDtypeStruct((SUBLANES, LANES), F32)),
        grid=(n // tm,),
        in_specs=[
            pl.BlockSpec((tm, d), rows),
            pl.BlockSpec((1, d), const),
            pl.BlockSpec((d, LANES), const),
            pl.BlockSpec((1, LANES), const),
        ],
        out_specs=(pl.BlockSpec((tm, d), rows),
                   pl.BlockSpec((tm, LANES), rows),
                   pl.BlockSpec((SUBLANES, LANES), const)),
        scratch_shapes=[pltpu.VMEM((SUBLANES, LANES), F32)],
        compiler_params=pltpu.CompilerParams(
            dimension_semantics=("arbitrary",), vmem_limit_bytes=VMEM_LIMIT),
        name="router",
    )(x2, norm2_g, w_r, b_r)


def _experts_kernel(te_ref, nu_ref, cur_ref, nxt_ref, h2_ref, w1_ref, b1_ref, w2_ref, b2_ref, y4_ref,
                    xbuf, ybuf, gsem, ssem):
    del te_ref
    i = pl.program_id(0)
    tm = xbuf.shape[1]
    de = w2_ref.shape[1]
    last_tok = h2_ref.shape[0] - 1
    n_used = nu_ref[0]

    def gather(idx_ref, slot):
        for r in range(tm):
            tok = jnp.minimum(idx_ref[0, 0, r] >> 2, last_tok)
            yield pltpu.make_async_copy(h2_ref.at[pl.ds(tok, 1), :], xbuf.at[slot, pl.ds(r, 1), :],
                                        gsem.at[slot])

    def scatter():
        for r in range(tm):
            yield pltpu.make_async_copy(ybuf.at[pl.ds(r, 1), :], y4_ref.at[pl.ds(cur_ref[0, 0, r], 1), :], ssem)

    @pl.when(i == 0)
    def _():
        for cp in gather(cur_ref, 0):
            cp.start()

    @pl.when(i < n_used)
    def _():
        slot = i % 2
        for cp in gather(nxt_ref, 1 - slot):
            cp.start()
        for cp in gather(cur_ref, slot):
            cp.wait()
        x = xbuf[slot].astype(BF16)
        h1 = jnp.dot(x, w1_ref[0].astype(BF16), preferred_element_type=F32) + b1_ref[0]
        glu = jnp.minimum(h1[:, :de], SWIGLU_LIMIT)
        lin = jnp.clip(h1[:, de:], -SWIGLU_LIMIT, SWIGLU_LIMIT)
        act = glu * _sigmoid(SWIGLU_ALPHA * glu) * (lin + 1.0)
        ybuf[...] = jnp.dot(act.astype(BF16), w2_ref[0].astype(BF16), preferred_element_type=F32) + b2_ref[0]
        for cp in scatter():
            cp.start()
        for cp in scatter():
            cp.wait()

    @pl.when(i == n_used - 1)
    def _():
        for cp in gather(nxt_ref, 1 - i % 2):
            cp.wait()


def _experts(tile_expert, n_used, dst, h2, w1, b1, w2, b2, *, tm):
    n, d = h2.shape
    n_tiles = dst.shape[0]
    de2 = w1.shape[2]
    de = w2.shape[1]
    wsel = lambda i, te, nu: (te[i], 0, 0)
    cur = lambda i, te, nu: (jnp.minimum(i, nu[0] - 1), 0, 0)
    nxt = lambda i, te, nu: (jnp.minimum(i + 1, nu[0] - 1), 0, 0)
    return pl.pallas_call(
        _experts_kernel,
        out_shape=jax.ShapeDtypeStruct((n_tiles * tm, d), F32),
        grid_spec=pltpu.PrefetchScalarGridSpec(
            num_scalar_prefetch=2,
            grid=(n_tiles,),
            in_specs=[pl.BlockSpec((1, 1, tm), cur, memory_space=pltpu.SMEM),
                      pl.BlockSpec((1, 1, tm), nxt, memory_space=pltpu.SMEM),
                      pl.BlockSpec(memory_space=pl.ANY),
                      pl.BlockSpec((1, d, de2), wsel),
                      pl.BlockSpec((1, 1, de2), wsel),
                      pl.BlockSpec((1, de, d), wsel),
                      pl.BlockSpec((1, 1, d), wsel)],
            out_specs=pl.BlockSpec(memory_space=pl.ANY),
            scratch_shapes=[pltpu.VMEM((2, tm, d), F32),
                            pltpu.VMEM((tm, d), F32),
                            pltpu.SemaphoreType.DMA((2,)),
                            pltpu.SemaphoreType.DMA],
        ),
        compiler_params=pltpu.CompilerParams(
            dimension_semantics=("arbitrary",), vmem_limit_bytes=VMEM_LIMIT, has_side_effects=True),
        name="experts",
    )(tile_expert, n_used, dst, dst, h2, w1, b1, w2, b2)


def _combine_kernel(x2_ref, meta_ref, fg_ref, y4_ref, out_ref, *, final_norm):
    d = x2_ref.shape[1]
    acc = x2_ref[...]
    meta = meta_ref[...]
    for kk in range(TOP_K):
        acc = acc + y4_ref[:, kk * d:(kk + 1) * d] * meta[:, TOP_K + kk:TOP_K + kk + 1]
    out_ref[...] = _rmsnorm(acc, fg_ref[...]) if final_norm else acc


def _combine(x2, meta, final_g, y4, *, tm, final_norm):
    n, d = x2.shape
    rows = lambda i: (i, 0)
    return pl.pallas_call(
        functools.partial(_combine_kernel, final_norm=final_norm),
        out_shape=jax.ShapeDtypeStruct((n, d), F32),
        grid=(n // tm,),
        in_specs=[pl.BlockSpec((tm, d), rows),
                  pl.BlockSpec((tm, LANES), rows),
                  pl.BlockSpec((1, d), lambda i: (0, 0)),
                  pl.BlockSpec((tm, TOP_K * d), rows)],
        out_specs=pl.BlockSpec((tm, d), rows),
        compiler_params=pltpu.CompilerParams(
            dimension_semantics=("parallel",), vmem_limit_bytes=VMEM_LIMIT),
        name="combine",
    )(x2, meta, final_g, y4)


def _pad_cols(w, cols):
    return jnp.pad(w, ((0, 0), (0, cols - w.shape[1])))


def _layer(x2d, p, *, batch, seq, tiles):
    d = x2d.shape[1]
    a_cols = 4 * A_WIDTH + 2 * A_HEADS
    b_cols = 3 * B_WIDTH + B_DECAY_LORA + B_ICLR_LORA + B_GATE_LORA
    w_in = p["w_in"]
    w_a = _pad_cols(w_in[:, :a_cols], 4 * A_WIDTH + LANES).astype(BF16)
    w_b = w_in[:, a_cols:a_cols + b_cols].astype(BF16)
    w_g = w_in[:, a_cols + b_cols:].astype(BF16)
    g1 = p["norm1_g"].reshape(1, d)

    a_par = jnp.zeros((SUBLANES, LANES), F32)
    a_par = a_par.at[0, A_HEADS:2 * A_HEADS].set(-jnp.exp(p["a_A_log"]))
    a_par = a_par.at[1, A_HEADS:2 * A_HEADS].set(p["a_dt_bias"])
    oa = _gdn(x2d, g1, w_a, p["a_conv_w"], a_par, p["a_norm_g"].reshape(1, A_HEAD_DIM),
              batch=batch, seq=seq, tt=tiles["tt_a"], chunk=A_CHUNK)

    w_lora = jnp.zeros((LANES, 2 * B_WIDTH), F32)
    w_lora = w_lora.at[:B_DECAY_LORA, :B_WIDTH].set(p["b_w2"])
    w_lora = w_lora.at[B_DECAY_LORA:, B_WIDTH:].set(p["b_a2"])
    vecs = jnp.stack([p["b_w0"], p["b_a0"], p["b_k_k"], p["b_k_a"], p["b_r_k"].reshape(-1),
                      p["b_ln_w"], p["b_ln_b"], jnp.zeros((B_WIDTH,), F32)])
    hd = jnp.arange(B_WIDTH) // B_HEAD_DIM
    hsum = (hd[:, None] == hd[None, :]).astype(BF16)
    ob = _rwkv(x2d, g1, w_b, p["b_mu"].reshape(1, b_cols), w_lora, p["b_g2"], vecs, hsum,
               batch=batch, seq=seq, tt=tiles["tt_b"], chunk=B_CHUNK)

    x2 = _merge(x2d, g1, oa, ob, w_g, p["w_up_a"].astype(BF16), p["w_up_b"].astype(BF16),
                p["w_out"].astype(BF16), tm=tiles["tm_merge"])

    w_r = _pad_cols(p["w_router"], LANES)
    b_r = jnp.full((1, LANES), -jnp.inf, F32).at[0, :N_EXPERTS].set(p["b_router"])
    h2, meta, cnt = _router(x2, p["norm2_g"].reshape(1, d), w_r, b_r, tm=tiles["tm_router"])

    tm_e = tiles["tm_expert"]
    n = x2d.shape[0]
    nk = n * TOP_K
    counts = cnt[0, :N_EXPERTS].astype(I32)
    tiles_per = (counts + tm_e - 1) // tm_e
    tile_end = jnp.cumsum(tiles_per)
    pad_off = (tile_end - tiles_per) * tm_e
    off = jnp.cumsum(counts) - counts
    n_tiles = nk // tm_e + N_EXPERTS
    n_used = tile_end[-1:]
    tile_ids = jnp.arange(n_tiles, dtype=I32)
    tile_expert = jnp.minimum(jnp.sum(tile_ids[:, None] >= tile_end[None, :], axis=1), N_EXPERTS - 1).astype(I32)
    last_e = tile_expert[jnp.maximum(n_used[0] - 1, 0)]
    tile_expert = jnp.where(tile_ids < n_used[0], tile_expert, last_e)
    order = jnp.argsort(meta[:, :TOP_K].astype(I32).reshape(-1), stable=True).astype(I32)
    rows = jnp.arange(n_tiles * tm_e, dtype=I32)
    row_e = jnp.repeat(tile_expert, tm_e)
    local = rows - pad_off[row_e]
    valid = (local < counts[row_e]) & (rows < n_used[0] * tm_e)
    flat = order[jnp.clip(off[row_e] + local, 0, nk - 1)]
    pad_rank = jnp.cumsum(jnp.logical_not(valid).astype(I32)) - 1
    dst = jnp.where(valid, flat, nk + pad_rank).reshape(n_tiles, 1, tm_e)

    y4 = _experts(tile_expert, n_used.astype(I32), dst, h2, p["w1"], p["b1"].reshape(N_EXPERTS, 1, -1),
                  p["w2"], p["b2"].reshape(N_EXPERTS, 1, -1), tm=tm_e)
    return x2, meta, y4.reshape(-1, TOP_K * d)


def _tiles(seq, n):
    return dict(tt_a=min(seq, 512), tt_b=min(seq, 512), tm_merge=min(n, 512), tm_router=min(n, 512),
                tm_expert=min(n, 512), tm_combine=min(n, 256))


def kernel(x, norm1_g, w_in, a_conv_w, a_A_log, a_dt_bias, a_norm_g, b_mu, b_w0, b_w2, b_a0, b_a2, b_g2, b_k_k, b_k_a, b_r_k, b_ln_w, b_ln_b, w_up_a, w_up_b, w_out, norm2_g, w_router, b_router, w1, b1, w2, b2, final_g):
    batch, seq, d = x.shape
    n = batch * seq
    depth = norm1_g.shape[0]
    tiles = _tiles(seq, n)
    params = dict(norm1_g=norm1_g, w_in=w_in, a_conv_w=a_conv_w, a_A_log=a_A_log, a_dt_bias=a_dt_bias,
                  a_norm_g=a_norm_g, b_mu=b_mu, b_w0=b_w0, b_w2=b_w2, b_a0=b_a0, b_a2=b_a2, b_g2=b_g2,
                  b_k_k=b_k_k, b_k_a=b_k_a, b_r_k=b_r_k, b_ln_w=b_ln_w, b_ln_b=b_ln_b, w_up_a=w_up_a,
                  w_up_b=w_up_b, w_out=w_out, norm2_g=norm2_g, w_router=w_router, b_router=b_router,
                  w1=w1, b1=b1, w2=w2, b2=b2)
    x2d = x.reshape(n, d)
    for l in range(depth):
        p = {k: v[l] for k, v in params.items()}
        x2, meta, y4 = _layer(x2d, p, batch=batch, seq=seq, tiles=tiles)
        x2d = _combine(x2, meta, final_g.reshape(1, d), y4, tm=tiles["tm_combine"],
                       final_norm=(l == depth - 1))
    return x2d.reshape(batch, seq, d)
```

```python
import functools

import jax
import jax.numpy as jnp
from jax import lax
from jax.experimental import pallas as pl
from jax.experimental.pallas import tpu as pltpu

F32 = jnp.float32
BF16 = jnp.bfloat16
I32 = jnp.int32

A_HEADS = 4
A_HEAD_DIM = 128
A_WIDTH = A_HEADS * A_HEAD_DIM
A_CONV = 4
B_HEADS = 8
B_HEAD_DIM = 64
B_WIDTH = B_HEADS * B_HEAD_DIM
B_DECAY_LORA = 64
B_ICLR_LORA = 64
B_GATE_LORA = 128
B_GN_EPS = 64e-5
N_EXPERTS = 32
TOP_K = 4
SWIGLU_ALPHA = 1.702
SWIGLU_LIMIT = 7.0
RMS_EPS = 1e-5
L2_EPS = 1e-6

LANES = 128
SUBLANES = 8
INV_BLOCK = 16
A_CHUNK = 128
B_CHUNK = 64
VMEM_LIMIT = 56 * 1024 * 1024


def _dot(a, b):
    return jnp.dot(a.astype(BF16), b.astype(BF16), preferred_element_type=F32)


def _dot_nt(a, b):
    return lax.dot_general(a.astype(BF16), b.astype(BF16), (((1,), (1,)), ((), ())),
                           preferred_element_type=F32)


def _dot_tn(a, b):
    return lax.dot_general(a.astype(BF16), b.astype(BF16), (((0,), (0,)), ((), ())),
                           preferred_element_type=F32)


def _dot_f32(a, b):
    return jnp.dot(a, b, preferred_element_type=F32, precision=lax.Precision.HIGHEST)


def _split_bf16(a, terms):
    out = []
    for _ in range(terms):
        hi = a.astype(BF16)
        out.append(hi)
        a = a - hi.astype(F32)
    return out


def _dot_x3(a, b):
    ah, al = _split_bf16(a, 2)
    bh, bl = _split_bf16(b, 2)
    mm = lambda u, v: jnp.dot(u, v, preferred_element_type=F32)
    return mm(ah, bh) + mm(ah, bl) + mm(al, bh)


def _dot_exact_lhs(a01, b):
    a01 = a01.astype(BF16)
    return sum(jnp.dot(a01, t, preferred_element_type=F32) for t in _split_bf16(b, 3))


def _sigmoid(x):
    return 1.0 / (1.0 + jnp.exp(-x))


def _silu(x):
    return x * _sigmoid(x)


def _softplus(x):
    return jnp.maximum(x, 0.0) + jnp.log(1.0 + jnp.exp(-jnp.abs(x)))


def _rmsnorm(x, g):
    return x * lax.rsqrt(jnp.mean(x * x, axis=-1, keepdims=True) + RMS_EPS) * g


def _iota2(shape, dim):
    return lax.broadcasted_iota(I32, shape, dim)


def _unit_lower_inverses(mats, unit):
    n = mats[0].shape[0]
    row = _iota2((n, n), 0)
    col = _iota2((n, n), 1)
    eye = (row == col).astype(F32)
    same_blk = (row // INV_BLOCK) == (col // INV_BLOCK)
    ps = [jnp.where(same_blk, -a, 0.0) for a in mats]
    dinv = [eye + x for x in ps]
    for _ in range(3):
        ps = [_dot(p, p) for p in ps]
        dinv = [dv + _dot(dv, p) for dv, p in zip(dinv, ps)]
    ps = [-_dot(dv, jnp.where(same_blk, 0.0, a)) for dv, a in zip(dinv, mats)]
    ts = [eye + y for y in ps]
    k = 2
    while k < unit // INV_BLOCK:
        ps = [_dot(p, p) for p in ps]
        ts = [t + _dot(t, p) for t, p in zip(ts, ps)]
        k *= 2
    return [_dot(t, dv) for t, dv in zip(ts, dinv)]


def _gdn_kernel(x_ref, g1_ref, wa_ref, convw_ref, apar_ref, ng_ref, oa_ref,
                ext_ref, s_ref, q_s, k_s, v_s, o_s, beta_s, gc_s, u_s, wq_s, attn_s, kg_s, egl_s, *, tt, chunk):
    qkv_w = 3 * A_WIDTH

    @pl.when(pl.program_id(1) == 0)
    def _():
        ext_ref[0:SUBLANES, :] = jnp.zeros((SUBLANES, qkv_w), F32)
        s_ref[...] = jnp.zeros(s_ref.shape, F32)

    h = _rmsnorm(x_ref[...], g1_ref[...]).astype(BF16)
    pa = jnp.dot(h, wa_ref[...], preferred_element_type=F32)

    ext_ref[SUBLANES:SUBLANES + tt, :] = pa[:, :qkv_w]
    conv = jnp.zeros((tt, qkv_w), F32)
    for j in range(A_CONV):
        off = SUBLANES - (A_CONV - 1) + j
        conv = conv + convw_ref[j:j + 1, :] * ext_ref[off:off + tt, :]
    ext_ref[0:SUBLANES, :] = ext_ref[tt:tt + SUBLANES, :]
    qkv = _silu(conv)

    for hd in range(A_HEADS):
        lo = hd * A_HEAD_DIM
        qh = qkv[:, lo:lo + A_HEAD_DIM]
        kh = qkv[:, A_WIDTH + lo:A_WIDTH + lo + A_HEAD_DIM]
        qn = qh * lax.rsqrt(jnp.sum(qh * qh, axis=-1, keepdims=True) + L2_EPS) * (A_HEAD_DIM ** -0.5)
        kn = kh * lax.rsqrt(jnp.sum(kh * kh, axis=-1, keepdims=True) + L2_EPS)
        q_s[:, lo:lo + A_HEAD_DIM] = qn
        k_s[:, lo:lo + A_HEAD_DIM] = kn
    v_s[...] = qkv[:, 2 * A_WIDTH:]

    bg = pa[:, 4 * A_WIDTH:]
    beta_s[...] = _sigmoid(bg)
    g_all = apar_ref[0:1, :] * _softplus(bg + apar_ref[1:2, :])
    ri = _iota2((chunk, chunk), 0)
    ci = _iota2((chunk, chunk), 1)
    causal = ci <= ri
    strict = ci < ri
    tri = causal.astype(F32)
    for c in range(tt // chunk):
        gc_s[c * chunk:(c + 1) * chunk, :] = _dot_exact_lhs(tri, g_all[c * chunk:(c + 1) * chunk, :])

    heads = range(A_HEADS)

    def prep_step(c, carry):
        r0 = pl.multiple_of(c * chunk, chunk)
        rows = pl.ds(r0, chunk)
        gcc = gc_s[rows, :]
        gct = gcc.T
        bc = beta_s[rows, :]
        cols = [slice(hd * A_HEAD_DIM, (hd + 1) * A_HEAD_DIM) for hd in heads]
        q = [q_s[rows, cs] for cs in cols]
        k = [k_s[rows, cs] for cs in cols]
        v = [v_s[rows, cs] for cs in cols]
        beta = [bc[:, hd:hd + 1] for hd in heads]
        gcol = [gcc[:, A_HEADS + hd:A_HEADS + hd + 1] for hd in heads]
        grow = [gct[A_HEADS + hd:A_HEADS + hd + 1, :] for hd in heads]
        glast = [gcc[chunk - 1:chunk, A_HEADS + hd:A_HEADS + hd + 1] for hd in heads]
        decay = [jnp.where(causal, jnp.exp(jnp.where(causal, gcol[hd] - grow[hd], 0.0)), 0.0) for hd in heads]
        eg = [jnp.exp(gcol[hd]) for hd in heads]
        kb = [k[hd] * beta[hd] for hd in heads]
        sc = [_dot_nt(jnp.concatenate([kb[hd], q[hd]], axis=0), k[hd]) for hd in heads]
        lmat = [jnp.where(strict, sc[hd][:chunk] * decay[hd], 0.0) for hd in heads]
        tmat = _unit_lower_inverses(lmat, chunk)
        uw = [_dot(tmat[hd], jnp.concatenate([v[hd] * beta[hd], kb[hd] * eg[hd]], axis=1)) for hd in heads]
        for hd in heads:
            j = c * A_HEADS + hd
            u_s[j] = uw[hd][:, :A_HEAD_DIM]
            wq_s[j] = jnp.concatenate([uw[hd][:, A_HEAD_DIM:], q[hd] * eg[hd]], axis=0).astype(BF16)
            attn_s[j] = (sc[hd][chunk:] * decay[hd]).astype(BF16)
            kg_s[j] = (k[hd] * jnp.exp(glast[hd] - gcol[hd])).astype(BF16)
            egl_s[j] = jnp.broadcast_to(jnp.exp(glast[hd]), (SUBLANES, LANES))
        return carry

    def scan_step(c, carry):
        r0 = pl.multiple_of(c * chunk, chunk)
        js = [c * A_HEADS + hd for hd in heads]
        s = [s_ref[hd] for hd in heads]
        ws = [_dot(wq_s[js[hd]], s[hd]) for hd in heads]
        v_new = [u_s[js[hd]] - ws[hd][:chunk] for hd in heads]
        o = [ws[hd][chunk:] + _dot(attn_s[js[hd]], v_new[hd]) for hd in heads]
        for hd in heads:
            s_ref[hd] = s[hd] * egl_s[js[hd]][0:1, :] + _dot_tn(kg_s[js[hd]], v_new[hd])
            o_s[pl.ds(r0, chunk), hd * A_HEAD_DIM:(hd + 1) * A_HEAD_DIM] = o[hd]
        return carry

    lax.fori_loop(0, tt // chunk, prep_step, 0)
    lax.fori_loop(0, tt // chunk, scan_step, 0)

    z = pa[:, 3 * A_WIDTH:4 * A_WIDTH]
    for hd in range(A_HEADS):
        lo = hd * A_HEAD_DIM
        o = o_s[:, lo:lo + A_HEAD_DIM]
        o = o * lax.rsqrt(jnp.mean(o * o, axis=-1, keepdims=True) + L2_EPS) * ng_ref[...]
        oa_ref[:, lo:lo + A_HEAD_DIM] = (o * _silu(z[:, lo:lo + A_HEAD_DIM])).astype(oa_ref.dtype)


def _gdn(x2d, norm1_g, w_a, conv_w, a_par, norm_g, *, batch, seq, tt, chunk):
    n, d = x2d.shape
    nt = seq // tt
    cols = w_a.shape[1]
    nj = tt // chunk * A_HEADS
    kern = functools.partial(_gdn_kernel, tt=tt, chunk=chunk)
    const = lambda b, t: (0, 0)
    return pl.pallas_call(
        kern,
        out_shape=jax.ShapeDtypeStruct((n, A_WIDTH), BF16),
        grid=(batch, nt),
        in_specs=[
            pl.BlockSpec((tt, d), lambda b, t: (b * nt + t, 0)),
            pl.BlockSpec((1, d), const),
            pl.BlockSpec((d, cols), const),
            pl.BlockSpec((A_CONV, 3 * A_WIDTH), const),
            pl.BlockSpec((SUBLANES, LANES), const),
            pl.BlockSpec((1, A_HEAD_DIM), const),
        ],
        out_specs=pl.BlockSpec((tt, A_WIDTH), lambda b, t: (b * nt + t, 0)),
        scratch_shapes=[
            pltpu.VMEM((tt + SUBLANES, 3 * A_WIDTH), F32),
            pltpu.VMEM((A_HEADS, A_HEAD_DIM, A_HEAD_DIM), F32),
            pltpu.VMEM((tt, A_WIDTH), F32),
            pltpu.VMEM((tt, A_WIDTH), F32),
            pltpu.VMEM((tt, A_WIDTH), F32),
            pltpu.VMEM((tt, A_WIDTH), F32),
            pltpu.VMEM((tt, LANES), F32),
            pltpu.VMEM((tt, LANES), F32),
            pltpu.VMEM((nj, chunk, A_HEAD_DIM), F32),
            pltpu.VMEM((nj, 2 * chunk, A_HEAD_DIM), BF16),
            pltpu.VMEM((nj, chunk, chunk), BF16),
            pltpu.VMEM((nj, chunk, A_HEAD_DIM), BF16),
            pltpu.VMEM((nj, SUBLANES, LANES), F32),
        ],
        compiler_params=pltpu.CompilerParams(
            dimension_semantics=("parallel", "arbitrary"), vmem_limit_bytes=VMEM_LIMIT),
        name="gdn",
    )(x2d, norm1_g, w_a, conv_w, a_par, norm_g)


def _rwkv_kernel(x_ref, g1_ref, wb_ref, mu_ref, wlora_ref, g2_ref, vec_ref, hsum_ref, ob_ref,
                 ext_ref, s_ref, ar_s, kb_s, hh_s, vs_s, av_s, arb_s, t_s, o_s, wtot_s, *, tt, chunk):
    bcols = 3 * B_WIDTH + B_DECAY_LORA + B_ICLR_LORA + B_GATE_LORA
    npairs = B_WIDTH // LANES

    @pl.when(pl.program_id(1) == 0)
    def _():
        ext_ref[0:SUBLANES, :] = jnp.zeros((SUBLANES, bcols), F32)
        s_ref[...] = jnp.zeros(s_ref.shape, F32)

    h = _rmsnorm(x_ref[...], g1_ref[...]).astype(BF16)
    pb = jnp.dot(h, wb_ref[...], preferred_element_type=F32)

    ext_ref[SUBLANES:SUBLANES + tt, :] = pb
    prev = ext_ref[SUBLANES - 1:SUBLANES - 1 + tt, :]
    ext_ref[0:SUBLANES, :] = ext_ref[tt:tt + SUBLANES, :]
    xm = pb + (prev - pb) * mu_ref[...]

    r = xm[:, 0:B_WIDTH]
    k = xm[:, B_WIDTH:2 * B_WIDTH]
    v = xm[:, 2 * B_WIDTH:3 * B_WIDTH]
    xwa = xm[:, 3 * B_WIDTH:3 * B_WIDTH + LANES]
    xg = xm[:, 3 * B_WIDTH + LANES:]
    lane = _iota2((tt, LANES), 1)
    lora_in = jnp.where(lane < B_DECAY_LORA, jnp.tanh(xwa), xwa)
    lora = _dot_x3(lora_in, wlora_ref[...])
    w0 = vec_ref[0:1, :]
    a0 = vec_ref[1:2, :]
    k_k = vec_ref[2:3, :]
    k_a = vec_ref[3:4, :]
    r_k = vec_ref[4:5, :]
    ln_w = vec_ref[5:6, :]
    ln_b = vec_ref[6:7, :]
    wl = -_softplus(-(w0 + lora[:, :B_WIDTH])) - 0.5
    lw = -jnp.exp(wl)
    a = _sigmoid(a0 + lora[:, B_WIDTH:])
    gate = _dot(_sigmoid(xg), g2_ref[...])

    hsum = hsum_ref[...]

    def head_sum(t):
        return jnp.dot(t.astype(BF16), hsum, preferred_element_type=F32)

    kk = k * k_k
    kk = kk * lax.rsqrt(head_sum(kk * kk) + L2_EPS)
    kf = k * (1.0 + (a - 1.0) * k_a)
    beta = kk * a

    lane2 = _iota2((chunk, LANES), 1)
    m0 = lane2 < B_HEAD_DIM

    def stack(*ts):
        parts = []
        for t in ts:
            parts += [jnp.where(m0, t, 0.0), jnp.where(m0, 0.0, t)]
        return jnp.concatenate(parts, axis=0).astype(BF16)

    tri = (_iota2((chunk, chunk), 1) <= _iota2((chunk, chunk), 0)).astype(F32)
    for c in range(tt // chunk):
        rs = slice(c * chunk, (c + 1) * chunk)
        lwc = lw[rs]
        gcum = _dot_exact_lhs(tri, lwc)
        gtot = gcum[chunk - 1:chunk, :]
        w_inv = jnp.exp(-gcum)
        w_rem = jnp.exp(gtot - gcum)
        at = kk[rs] * jnp.exp(gcum - lwc)
        kt = kf[rs] * w_inv
        bt = beta[rs] * w_inv
        rt = r[rs] * jnp.exp(gcum)
        kh = kf[rs] * w_rem
        bh = beta[rs] * w_rem
        vc = v[rs]
        wtot = jnp.broadcast_to(jnp.exp(gtot), (SUBLANES, B_WIDTH))
        for p in range(npairs):
            ls = slice(p * LANES, (p + 1) * LANES)
            j = c * npairs + p
            ar_s[j] = stack(at[:, ls], rt[:, ls])
            kb_s[j] = stack(kt[:, ls], bt[:, ls])
            hh_s[j] = stack(kh[:, ls], bh[:, ls])
            vs_s[j] = stack(vc[:, ls])
            wtot_s[j] = wtot[:, ls]

    n2 = 2 * chunk
    ri = _iota2((n2, n2), 0)
    ci = _iota2((n2, n2), 1)
    incl = ci <= ri
    strict = ci < ri
    pairs = range(npairs)

    def prep_step(c, carry):
        js = [c * npairs + p for p in pairs]
        sc = [_dot_nt(ar_s[j], kb_s[j]) for j in js]
        a_ab = [jnp.where(strict, s4[:n2, n2:], 0.0) for s4 in sc]
        tmat = _unit_lower_inverses(a_ab, chunk)
        for p in pairs:
            j = js[p]
            s4 = sc[p]
            a_kk = jnp.concatenate([jnp.where(strict, s4[:n2, :n2], 0.0),
                                    jnp.where(incl, s4[n2:, :n2], 0.0)], axis=0)
            av_s[j] = _dot(a_kk, vs_s[j])
            arb_s[j] = jnp.where(incl, s4[n2:, n2:], 0.0).astype(BF16)
            t_s[j] = tmat[p].astype(BF16)
        return carry

    def scan_step(c, carry):
        r0 = pl.multiple_of(c * chunk, chunk)
        js = [c * npairs + p for p in pairs]
        s = [s_ref[p] for p in pairs]
        ps = [_dot_nt(ar_s[js[p]], s[p]) + av_s[js[p]] for p in pairs]
        u = [_dot(t_s[js[p]], ps[p][:n2]) for p in pairs]
        o2 = [ps[p][n2:] - _dot(arb_s[js[p]], u[p]) for p in pairs]
        for p in pairs:
            vu = jnp.concatenate([vs_s[js[p]], (-u[p]).astype(BF16)], axis=0)
            s_ref[p] = s[p] * wtot_s[js[p]][0:1, :] + _dot_tn(vu, hh_s[js[p]])
            o_s[pl.ds(r0, chunk), p * LANES:(p + 1) * LANES] = o2[p][:chunk] + o2[p][chunk:]
        return carry

    lax.fori_loop(0, tt // chunk, prep_step, 0)
    lax.fori_loop(0, tt // chunk, scan_step, 0)

    out = o_s[...]
    mean = head_sum(out) * (1.0 / B_HEAD_DIM)
    cen = out - mean
    var = head_sum(cen * cen) * (1.0 / B_HEAD_DIM)
    out = cen * lax.rsqrt(var + B_GN_EPS) * ln_w + ln_b
    bonus = head_sum(r * kf * r_k) * v
    ob_ref[...] = ((out + bonus) * gate).astype(ob_ref.dtype)


def _rwkv(x2d, norm1_g, w_b, mu, w_lora, g2, vecs, hsum, *, batch, seq, tt, chunk):
    n, d = x2d.shape
    nt = seq // tt
    bcols = w_b.shape[1]
    kern = functools.partial(_rwkv_kernel, tt=tt, chunk=chunk)
    const = lambda b, t: (0, 0)
    nj = tt // chunk * (B_WIDTH // LANES)
    return pl.pallas_call(
        kern,
        out_shape=jax.ShapeDtypeStruct((n, B_WIDTH), BF16),
        grid=(batch, nt),
        in_specs=[
            pl.BlockSpec((tt, d), lambda b, t: (b * nt + t, 0)),
            pl.BlockSpec((1, d), const),
            pl.BlockSpec((d, bcols), const),
            pl.BlockSpec((1, bcols), const),
            pl.BlockSpec((LANES, 2 * B_WIDTH), const),
            pl.BlockSpec((B_GATE_LORA, B_WIDTH), const),
            pl.BlockSpec((SUBLANES, B_WIDTH), const),
            pl.BlockSpec((B_WIDTH, B_WIDTH), const),
        ],
        out_specs=pl.BlockSpec((tt, B_WIDTH), lambda b, t: (b * nt + t, 0)),
        scratch_shapes=[
            pltpu.VMEM((tt + SUBLANES, bcols), F32),
            pltpu.VMEM((B_WIDTH // LANES, LANES, LANES), F32),
            pltpu.VMEM((nj, 4 * chunk, LANES), BF16),
            pltpu.VMEM((nj, 4 * chunk, LANES), BF16),
            pltpu.VMEM((nj, 4 * chunk, LANES), BF16),
            pltpu.VMEM((nj, 2 * chunk, LANES), BF16),
            pltpu.VMEM((nj, 4 * chunk, LANES), F32),
            pltpu.VMEM((nj, 2 * chunk, 2 * chunk), BF16),
            pltpu.VMEM((nj, 2 * chunk, 2 * chunk), BF16),
            pltpu.VMEM((tt, B_WIDTH), F32),
            pltpu.VMEM((nj, SUBLANES, LANES), F32),
        ],
        compiler_params=pltpu.CompilerParams(
            dimension_semantics=("parallel", "arbitrary"), vmem_limit_bytes=VMEM_LIMIT),
        name="rwkv",
    )(x2d, norm1_g, w_b, mu, w_lora, g2, vecs, hsum)


def _merge_kernel(x_ref, g1_ref, oa_ref, ob_ref, wg_ref, wua_ref, wub_ref, wo_ref, x2_ref):
    d = x_ref.shape[1]
    x = x_ref[...]
    h = _rmsnorm(x, g1_ref[...]).astype(BF16)
    pg = jnp.dot(h, wg_ref[...], preferred_element_type=F32)
    ya = jnp.dot(oa_ref[...], wua_ref[...], preferred_element_type=F32)
    yb = jnp.dot(ob_ref[...], wub_ref[...], preferred_element_type=F32)
    merged = _sigmoid(pg[:, :d]) * ya + _sigmoid(pg[:, d:]) * yb
    x2_ref[...] = x + jnp.dot(merged.astype(BF16), wo_ref[...], preferred_element_type=F32)


def _merge(x2d, norm1_g, oa, ob, w_g, w_up_a, w_up_b, w_out, *, tm):
    n, d = x2d.shape
    const = lambda i: (0, 0)
    rows = lambda i: (i, 0)
    return pl.pallas_call(
        _merge_kernel,
        out_shape=jax.ShapeDtypeStruct((n, d), F32),
        grid=(n // tm,),
        in_specs=[
            pl.BlockSpec((tm, d), rows),
            pl.BlockSpec((1, d), const),
            pl.BlockSpec((tm, A_WIDTH), rows),
            pl.BlockSpec((tm, B_WIDTH), rows),
            pl.BlockSpec((d, 2 * d), const),
            pl.BlockSpec((A_WIDTH, d), const),
            pl.BlockSpec((B_WIDTH, d), const),
            pl.BlockSpec((d, d), const),
        ],
        out_specs=pl.BlockSpec((tm, d), rows),
        compiler_params=pltpu.CompilerParams(
            dimension_semantics=("parallel",), vmem_limit_bytes=VMEM_LIMIT),
        name="merge",
    )(x2d, norm1_g, oa, ob, w_g, w_up_a, w_up_b, w_out)


def _router_kernel(x2_ref, g2_ref, wr_ref, br_ref, h2_ref, meta_ref, cnt_ref, run_ref):
    tm = x2_ref.shape[0]

    @pl.when(pl.program_id(0) == 0)
    def _():
        run_ref[...] = jnp.zeros(run_ref.shape, F32)

    h2 = _rmsnorm(x2_ref[...], g2_ref[...])
    for c in range(h2_ref.shape[1]):
        h2_ref[:, c, :] = h2[:, c * LANES:(c + 1) * LANES]
    logits = _dot_f32(h2, wr_ref[...]) + br_ref[...]
    lane = _iota2((tm, LANES), 1)
    neg = jnp.float32(-jnp.inf)

    work = logits
    sel_idx = []
    sel_val = []
    multi = jnp.zeros((tm, LANES), F32)
    for _ in range(TOP_K):
        mx = jnp.max(work, axis=-1, keepdims=True)
        idx = jnp.min(jnp.where(work == mx, lane, LANES), axis=-1, keepdims=True)
        hit = lane == idx
        multi = multi + hit.astype(F32)
        work = jnp.where(hit, neg, work)
        sel_idx.append(idx)
        sel_val.append(mx)
    ex = [jnp.exp(vv - sel_val[0]) for vv in sel_val]
    den = ex[0] + ex[1] + ex[2] + ex[3]
    gates = [e / den for e in ex]

    run_ref[0:1, :] = run_ref[0:1, :] + jnp.sum(multi, axis=0, keepdims=True)
    cnt_ref[...] = run_ref[...]

    meta = jnp.zeros((tm, LANES), F32)
    for kk in range(TOP_K):
        meta = jnp.where(lane == kk, sel_idx[kk].astype(F32), meta)
        meta = jnp.where(lane == TOP_K + kk, gates[kk], meta)
    meta_ref[...] = meta


def _router(x2, norm2_g, w_r, b_r, *, tm):
    n, d = x2.shape
    const = lambda i: (0, 0)
    rows = lambda i: (i, 0)
    return pl.pallas_call(
        _router_kernel,
        out_shape=(jax.ShapeDtypeStruct((n, d // LANES, LANES), F32),
                   jax.ShapeDtypeStruct((n, LANES), F32),
                   jax.ShapeDtypeStruct((SUBLANES, LANES), F32)),
        grid=(n // tm,),
        in_specs=[
            pl.BlockSpec((tm, d), rows),
            pl.BlockSpec((1, d), const),
            pl.BlockSpec((d, LANES), const),
            pl.BlockSpec((1, LANES), const),
        ],
        out_specs=(pl.BlockSpec((tm, d // LANES, LANES), lambda i: (i, 0, 0)),
                   pl.BlockSpec((tm, LANES), rows),
                   pl.BlockSpec((SUBLANES, LANES), const)),
        scratch_shapes=[pltpu.VMEM((SUBLANES, LANES), F32)],
        compiler_params=pltpu.CompilerParams(
            dimension_semantics=("arbitrary",), vmem_limit_bytes=VMEM_LIMIT),
        name="router",
    )(x2, norm2_g, w_r, b_r)


def _experts_kernel(te_ref, nu_ref, cur_ref, nxt_ref, h2_ref, w1_ref, b1_ref, w2_ref, b2_ref, y4_ref,
                    xbuf, ybuf, gsem, ssem):
    del te_ref
    i = pl.program_id(0)
    tm = xbuf.shape[1]
    nc = xbuf.shape[2]
    de = w2_ref.shape[1]
    n_tok = h2_ref.shape[0]
    n_used = nu_ref[0]
    unroll = 8

    def token_of(dst_row):
        return dst_row & (n_tok - 1) if n_tok & (n_tok - 1) == 0 else lax.rem(dst_row, n_tok)

    def gather_row(idx_ref, slot, r):
        return pltpu.make_async_copy(h2_ref.at[token_of(idx_ref[0, 0, r])], xbuf.at[slot, r], gsem.at[slot])

    def gather_wait_all(slot):
        for r in range(tm):
            pltpu.make_async_copy(h2_ref.at[0], xbuf.at[slot, r], gsem.at[slot]).wait()

    def scatter_row(slot, r):
        return pltpu.make_async_copy(ybuf.at[slot, r], y4_ref.at[cur_ref[0, 0, r]], ssem.at[slot])

    def scatter_wait_all(slot):
        for r in range(tm):
            pltpu.make_async_copy(ybuf.at[slot, r], y4_ref.at[0], ssem.at[slot]).wait()

    def rows_loop(fn):
        def body(g, carry):
            for u in range(unroll):
                fn(g * unroll + u)
            return carry
        lax.fori_loop(0, tm // unroll, body, 0)

    @pl.when(i == 0)
    def _():
        rows_loop(lambda r: gather_row(cur_ref, 0, r).start())

    @pl.when(jnp.logical_and(i >= 2, i < n_used))
    def _():
        scatter_wait_all(i % 2)

    @pl.when(i < n_used)
    def _():
        slot = i % 2
        gather_wait_all(slot)
        x = jnp.concatenate([xbuf[slot, :, c, :] for c in range(nc)], axis=1).astype(BF16)
        nblk = 4
        bw = de // nblk
        acts = []
        for j in range(nblk):
            for r in range(j * tm // nblk, (j + 1) * tm // nblk):
                gather_row(nxt_ref, 1 - slot, r).start()
            cg = slice(j * bw, (j + 1) * bw)
            cl = slice(de + j * bw, de + (j + 1) * bw)
            hg = jnp.dot(x, w1_ref[0, :, cg].astype(BF16), preferred_element_type=F32) + b1_ref[0, :, cg]
            hl = jnp.dot(x, w1_ref[0, :, cl].astype(BF16), preferred_element_type=F32) + b1_ref[0, :, cl]
            glu = jnp.minimum(hg, SWIGLU_LIMIT)
            lin = jnp.clip(hl, -SWIGLU_LIMIT, SWIGLU_LIMIT)
            acts.append((glu * _sigmoid(SWIGLU_ALPHA * glu) * (lin + 1.0)).astype(BF16))
        act = jnp.concatenate(acts, axis=1)
        y = jnp.dot(act, w2_ref[0].astype(BF16), preferred_element_type=F32) + b2_ref[0]
        for c in range(nc):
            ybuf[slot, :, c, :] = y[:, c * LANES:(c + 1) * LANES]
        rows_loop(lambda r: scatter_row(slot, r).start())

    @pl.when(i == n_used - 1)
    def _():
        scatter_wait_all(i % 2)
        gather_wait_all(1 - i % 2)

    @pl.when(jnp.logical_and(i == n_used - 1, i >= 1))
    def _():
        scatter_wait_all(1 - i % 2)


def _experts(tile_expert, n_used, dst, h2, w1, b1, w2, b2, *, tm):
    n, nc, _ = h2.shape
    d = nc * LANES
    n_tiles = dst.shape[0]
    de2 = w1.shape[2]
    de = w2.shape[1]
    wsel = lambda i, te, nu: (te[i], 0, 0)
    cur = lambda i, te, nu: (jnp.minimum(i, nu[0] - 1), 0, 0)
    nxt = lambda i, te, nu: (jnp.minimum(i + 1, nu[0] - 1), 0, 0)
    return pl.pallas_call(
        _experts_kernel,
        out_shape=jax.ShapeDtypeStruct((n_tiles * tm, nc, LANES), F32),
        grid_spec=pltpu.PrefetchScalarGridSpec(
            num_scalar_prefetch=2,
            grid=(n_tiles,),
            in_specs=[pl.BlockSpec((1, 1, tm), cur, memory_space=pltpu.SMEM),
                      pl.BlockSpec((1, 1, tm), nxt, memory_space=pltpu.SMEM),
                      pl.BlockSpec(memory_space=pl.ANY),
                      pl.BlockSpec((1, d, de2), wsel),
                      pl.BlockSpec((1, 1, de2), wsel),
                      pl.BlockSpec((1, de, d), wsel),
                      pl.BlockSpec((1, 1, d), wsel)],
            out_specs=pl.BlockSpec(memory_space=pl.ANY),
            scratch_shapes=[pltpu.VMEM((2, tm, nc, LANES), F32),
                            pltpu.VMEM((2, tm, nc, LANES), F32),
                            pltpu.SemaphoreType.DMA((2,)),
                            pltpu.SemaphoreType.DMA((2,))],
        ),
        compiler_params=pltpu.CompilerParams(
            dimension_semantics=("arbitrary",), vmem_limit_bytes=VMEM_LIMIT, has_side_effects=True),
        name="experts",
    )(tile_expert, n_used, dst, dst, h2, w1, b1, w2, b2)


def _combine_kernel(x2_ref, meta_ref, fg_ref, *refs, final_norm):
    y_refs, out_ref = refs[:TOP_K], refs[TOP_K]
    nc = y_refs[0].shape[1]
    meta = meta_ref[...]
    gates = [meta[:, TOP_K + kk:TOP_K + kk + 1] for kk in range(TOP_K)]
    cols = []
    for c in range(nc):
        acc = x2_ref[:, c * LANES:(c + 1) * LANES]
        for kk in range(TOP_K):
            acc = acc + y_refs[kk][:, c, :] * gates[kk]
        cols.append(acc)
    acc = jnp.concatenate(cols, axis=1)
    out_ref[...] = _rmsnorm(acc, fg_ref[...]) if final_norm else acc


def _combine(x2, meta, final_g, y4, *, tm, final_norm):
    n, d = x2.shape
    nc = y4.shape[1]
    nb = n // tm
    rows = lambda i: (i, 0)
    slot_rows = lambda kk: (lambda i: (kk * nb + i, 0, 0))
    return pl.pallas_call(
        functools.partial(_combine_kernel, final_norm=final_norm),
        out_shape=jax.ShapeDtypeStruct((n, d), F32),
        grid=(nb,),
        in_specs=[pl.BlockSpec((tm, d), rows),
                  pl.BlockSpec((tm, LANES), rows),
                  pl.BlockSpec((1, d), lambda i: (0, 0))]
                 + [pl.BlockSpec((tm, nc, LANES), slot_rows(kk)) for kk in range(TOP_K)],
        out_specs=pl.BlockSpec((tm, d), rows),
        compiler_params=pltpu.CompilerParams(
            dimension_semantics=("parallel",), vmem_limit_bytes=VMEM_LIMIT),
        name="combine",
    )(x2, meta, final_g, *([y4] * TOP_K))


def _pad_cols(w, cols):
    return jnp.pad(w, ((0, 0), (0, cols - w.shape[1])))


def _layer(x2d, p, *, batch, seq, tiles):
    d = x2d.shape[1]
    a_cols = 4 * A_WIDTH + 2 * A_HEADS
    b_cols = 3 * B_WIDTH + B_DECAY_LORA + B_ICLR_LORA + B_GATE_LORA
    w_in = p["w_in"]
    w_a = _pad_cols(w_in[:, :a_cols], 4 * A_WIDTH + LANES).astype(BF16)
    w_b = w_in[:, a_cols:a_cols + b_cols].astype(BF16)
    w_g = w_in[:, a_cols + b_cols:].astype(BF16)
    g1 = p["norm1_g"].reshape(1, d)

    a_par = jnp.zeros((SUBLANES, LANES), F32)
    a_par = a_par.at[0, A_HEADS:2 * A_HEADS].set(-jnp.exp(p["a_A_log"]))
    a_par = a_par.at[1, A_HEADS:2 * A_HEADS].set(p["a_dt_bias"])
    oa = _gdn(x2d, g1, w_a, p["a_conv_w"], a_par, p["a_norm_g"].reshape(1, A_HEAD_DIM),
              batch=batch, seq=seq, tt=tiles["tt_a"], chunk=A_CHUNK)

    w_lora = jnp.zeros((LANES, 2 * B_WIDTH), F32)
    w_lora = w_lora.at[:B_DECAY_LORA, :B_WIDTH].set(p["b_w2"])
    w_lora = w_lora.at[B_DECAY_LORA:, B_WIDTH:].set(p["b_a2"])
    vecs = jnp.stack([p["b_w0"], p["b_a0"], p["b_k_k"], p["b_k_a"], p["b_r_k"].reshape(-1),
                      p["b_ln_w"], p["b_ln_b"], jnp.zeros((B_WIDTH,), F32)])
    hd = jnp.arange(B_WIDTH) // B_HEAD_DIM
    hsum = (hd[:, None] == hd[None, :]).astype(BF16)
    ob = _rwkv(x2d, g1, w_b, p["b_mu"].reshape(1, b_cols), w_lora, p["b_g2"], vecs, hsum,
               batch=batch, seq=seq, tt=tiles["tt_b"], chunk=B_CHUNK)

    x2 = _merge(x2d, g1, oa, ob, w_g, p["w_up_a"].astype(BF16), p["w_up_b"].astype(BF16),
                p["w_out"].astype(BF16), tm=tiles["tm_merge"])

    w_r = _pad_cols(p["w_router"], LANES)
    b_r = jnp.full((1, LANES), -jnp.inf, F32).at[0, :N_EXPERTS].set(p["b_router"])
    h2, meta, cnt = _router(x2, p["norm2_g"].reshape(1, d), w_r, b_r, tm=tiles["tm_router"])

    tm_e = tiles["tm_expert"]
    n = x2d.shape[0]
    nk = n * TOP_K
    counts = cnt[0, :N_EXPERTS].astype(I32)
    tiles_per = (counts + tm_e - 1) // tm_e
    tile_end = jnp.cumsum(tiles_per)
    pad_off = (tile_end - tiles_per) * tm_e
    off = jnp.cumsum(counts) - counts
    n_tiles = nk // tm_e + N_EXPERTS
    n_used = tile_end[-1:]
    tile_ids = jnp.arange(n_tiles, dtype=I32)
    tile_expert = jnp.minimum(jnp.sum(tile_ids[:, None] >= tile_end[None, :], axis=1), N_EXPERTS - 1).astype(I32)
    last_e = tile_expert[jnp.maximum(n_used[0] - 1, 0)]
    tile_expert = jnp.where(tile_ids < n_used[0], tile_expert, last_e)
    order = jnp.argsort(meta[:, :TOP_K].astype(I32).reshape(-1), stable=True).astype(I32)
    rows = jnp.arange(n_tiles * tm_e, dtype=I32)
    row_e = jnp.repeat(tile_expert, tm_e)
    local = rows - pad_off[row_e]
    valid = (local < counts[row_e]) & (rows < n_used[0] * tm_e)
    flat = order[jnp.clip(off[row_e] + local, 0, nk - 1)]
    pad_rank = jnp.cumsum(jnp.logical_not(valid).astype(I32)) - 1
    dst = jnp.where(valid, (flat % TOP_K) * n + flat // TOP_K, nk + pad_rank).reshape(n_tiles, 1, tm_e)

    y4 = _experts(tile_expert, n_used.astype(I32), dst, h2, p["w1"], p["b1"].reshape(N_EXPERTS, 1, -1),
                  p["w2"], p["b2"].reshape(N_EXPERTS, 1, -1), tm=tm_e)
    return x2, meta, y4


def _tiles(seq, n):
    return dict(tt_a=min(seq, 512), tt_b=min(seq, 512), tm_merge=min(n, 512), tm_router=min(n, 512),
                tm_expert=min(n, 512), tm_combine=min(n, 256))


def kernel(x, norm1_g, w_in, a_conv_w, a_A_log, a_dt_bias, a_norm_g, b_mu, b_w0, b_w2, b_a0, b_a2, b_g2, b_k_k, b_k_a, b_r_k, b_ln_w, b_ln_b, w_up_a, w_up_b, w_out, norm2_g, w_router, b_router, w1, b1, w2, b2, final_g):
    batch, seq, d = x.shape
    n = batch * seq
    depth = norm1_g.shape[0]
    tiles = _tiles(seq, n)
    params = dict(norm1_g=norm1_g, w_in=w_in, a_conv_w=a_conv_w, a_A_log=a_A_log, a_dt_bias=a_dt_bias,
                  a_norm_g=a_norm_g, b_mu=b_mu, b_w0=b_w0, b_w2=b_w2, b_a0=b_a0, b_a2=b_a2, b_g2=b_g2,
                  b_k_k=b_k_k, b_k_a=b_k_a, b_r_k=b_r_k, b_ln_w=b_ln_w, b_ln_b=b_ln_b, w_up_a=w_up_a,
                  w_up_b=w_up_b, w_out=w_out, norm2_g=norm2_g, w_router=w_router, b_router=b_router,
                  w1=w1, b1=b1, w2=w2, b2=b2)
    x2d = x.reshape(n, d)
    for l in range(depth):
        p = {k: v[l] for k, v in params.items()}
        x2, meta, y4 = _layer(x2d, p, batch=batch, seq=seq, tiles=tiles)
        x2d = _combine(x2, meta, final_g.reshape(1, d), y4, tm=tiles["tm_combine"],
                       final_norm=(l == depth - 1))
    return x2d.reshape(batch, seq, d)
```

```python
import functools

import jax
import jax.numpy as jnp
from jax import lax
from jax.experimental import pallas as pl
from jax.experimental.pallas import tpu as pltpu

F32 = jnp.float32
BF16 = jnp.bfloat16
I32 = jnp.int32

A_HEADS = 4
A_HEAD_DIM = 128
A_WIDTH = A_HEADS * A_HEAD_DIM
A_CONV = 4
B_HEADS = 8
B_HEAD_DIM = 64
B_WIDTH = B_HEADS * B_HEAD_DIM
B_DECAY_LORA = 64
B_ICLR_LORA = 64
B_GATE_LORA = 128
B_GN_EPS = 64e-5
N_EXPERTS = 32
TOP_K = 4
SWIGLU_ALPHA = 1.702
SWIGLU_LIMIT = 7.0
RMS_EPS = 1e-5
L2_EPS = 1e-6

LANES = 128
SUBLANES = 8
INV_BLOCK = 16
A_CHUNK = 128
B_CHUNK = 64
VMEM_LIMIT = 56 * 1024 * 1024
DMA_QUEUES = 2


def _dot(a, b):
    return jnp.dot(a.astype(BF16), b.astype(BF16), preferred_element_type=F32)


def _dot_nt(a, b):
    return lax.dot_general(a.astype(BF16), b.astype(BF16), (((1,), (1,)), ((), ())),
                           preferred_element_type=F32)


def _dot_tn(a, b):
    return lax.dot_general(a.astype(BF16), b.astype(BF16), (((0,), (0,)), ((), ())),
                           preferred_element_type=F32)


def _dot_f32(a, b):
    return jnp.dot(a, b, preferred_element_type=F32, precision=lax.Precision.HIGHEST)


def _split_bf16(a, terms):
    out = []
    for _ in range(terms):
        hi = a.astype(BF16)
        out.append(hi)
        a = a - hi.astype(F32)
    return out


def _dot_x3(a, b):
    ah, al = _split_bf16(a, 2)
    bh, bl = _split_bf16(b, 2)
    mm = lambda u, v: jnp.dot(u, v, preferred_element_type=F32)
    return mm(ah, bh) + mm(ah, bl) + mm(al, bh)


def _dot_exact_lhs(a01, b):
    a01 = a01.astype(BF16)
    return sum(jnp.dot(a01, t, preferred_element_type=F32) for t in _split_bf16(b, 3))


def _sigmoid(x):
    return 1.0 / (1.0 + jnp.exp(-x))


def _silu(x):
    return x * _sigmoid(x)


def _softplus(x):
    return jnp.maximum(x, 0.0) + jnp.log(1.0 + jnp.exp(-jnp.abs(x)))


def _rmsnorm(x, g):
    return x * lax.rsqrt(jnp.mean(x * x, axis=-1, keepdims=True) + RMS_EPS) * g


def _iota2(shape, dim):
    return lax.broadcasted_iota(I32, shape, dim)


def _unit_lower_inverses(mats, unit):
    n = mats[0].shape[0]
    row = _iota2((n, n), 0)
    col = _iota2((n, n), 1)
    eye = (row == col).astype(F32)
    same_blk = (row // INV_BLOCK) == (col // INV_BLOCK)
    ps = [jnp.where(same_blk, -a, 0.0) for a in mats]
    dinv = [eye + x for x in ps]
    for _ in range(3):
        ps = [_dot(p, p) for p in ps]
        dinv = [dv + _dot(dv, p) for dv, p in zip(dinv, ps)]
    ps = [-_dot(dv, jnp.where(same_blk, 0.0, a)) for dv, a in zip(dinv, mats)]
    ts = [eye + y for y in ps]
    k = 2
    while k < unit // INV_BLOCK:
        ps = [_dot(p, p) for p in ps]
        ts = [t + _dot(t, p) for t, p in zip(ts, ps)]
        k *= 2
    return [_dot(t, dv) for t, dv in zip(ts, dinv)]


def _gdn_kernel(x_ref, g1_ref, wa_ref, convw_ref, apar_ref, ng_ref, oa_ref,
                ext_ref, s_ref, q_s, k_s, v_s, o_s, beta_s, gc_s, u_s, wq_s, attn_s, kg_s, egl_s, *, tt, chunk):
    qkv_w = 3 * A_WIDTH

    @pl.when(pl.program_id(1) == 0)
    def _():
        ext_ref[0:SUBLANES, :] = jnp.zeros((SUBLANES, qkv_w), F32)
        s_ref[...] = jnp.zeros(s_ref.shape, F32)

    h = _rmsnorm(x_ref[...], g1_ref[...]).astype(BF16)
    pa = jnp.dot(h, wa_ref[...], preferred_element_type=F32)

    ext_ref[SUBLANES:SUBLANES + tt, :] = pa[:, :qkv_w]
    conv = jnp.zeros((tt, qkv_w), F32)
    for j in range(A_CONV):
        off = SUBLANES - (A_CONV - 1) + j
        conv = conv + convw_ref[j:j + 1, :] * ext_ref[off:off + tt, :]
    ext_ref[0:SUBLANES, :] = ext_ref[tt:tt + SUBLANES, :]
    qkv = _silu(conv)

    for hd in range(A_HEADS):
        lo = hd * A_HEAD_DIM
        qh = qkv[:, lo:lo + A_HEAD_DIM]
        kh = qkv[:, A_WIDTH + lo:A_WIDTH + lo + A_HEAD_DIM]
        qn = qh * lax.rsqrt(jnp.sum(qh * qh, axis=-1, keepdims=True) + L2_EPS) * (A_HEAD_DIM ** -0.5)
        kn = kh * lax.rsqrt(jnp.sum(kh * kh, axis=-1, keepdims=True) + L2_EPS)
        q_s[:, lo:lo + A_HEAD_DIM] = qn
        k_s[:, lo:lo + A_HEAD_DIM] = kn
    v_s[...] = qkv[:, 2 * A_WIDTH:]

    bg = pa[:, 4 * A_WIDTH:]
    beta_s[...] = _sigmoid(bg)
    g_all = apar_ref[0:1, :] * _softplus(bg + apar_ref[1:2, :])
    ri = _iota2((chunk, chunk), 0)
    ci = _iota2((chunk, chunk), 1)
    causal = ci <= ri
    strict = ci < ri
    tri = causal.astype(F32)
    for c in range(tt // chunk):
        gc_s[c * chunk:(c + 1) * chunk, :] = _dot_exact_lhs(tri, g_all[c * chunk:(c + 1) * chunk, :])

    heads = range(A_HEADS)

    def prep_step(c, carry):
        r0 = pl.multiple_of(c * chunk, chunk)
        rows = pl.ds(r0, chunk)
        gcc = gc_s[rows, :]
        gct = gcc.T
        bc = beta_s[rows, :]
        cols = [slice(hd * A_HEAD_DIM, (hd + 1) * A_HEAD_DIM) for hd in heads]
        q = [q_s[rows, cs] for cs in cols]
        k = [k_s[rows, cs] for cs in cols]
        v = [v_s[rows, cs] for cs in cols]
        beta = [bc[:, hd:hd + 1] for hd in heads]
        gcol = [gcc[:, A_HEADS + hd:A_HEADS + hd + 1] for hd in heads]
        grow = [gct[A_HEADS + hd:A_HEADS + hd + 1, :] for hd in heads]
        glast = [gcc[chunk - 1:chunk, A_HEADS + hd:A_HEADS + hd + 1] for hd in heads]
        decay = [jnp.where(causal, jnp.exp(jnp.where(causal, gcol[hd] - grow[hd], 0.0)), 0.0) for hd in heads]
        eg = [jnp.exp(gcol[hd]) for hd in heads]
        kb = [k[hd] * beta[hd] for hd in heads]
        sc = [_dot_nt(jnp.concatenate([kb[hd], q[hd]], axis=0), k[hd]) for hd in heads]
        lmat = [jnp.where(strict, sc[hd][:chunk] * decay[hd], 0.0) for hd in heads]
        tmat = _unit_lower_inverses(lmat, chunk)
        uw = [_dot(tmat[hd], jnp.concatenate([v[hd] * beta[hd], kb[hd] * eg[hd]], axis=1)) for hd in heads]
        for hd in heads:
            j = c * A_HEADS + hd
            u_s[j] = uw[hd][:, :A_HEAD_DIM]
            wq_s[j] = jnp.concatenate([uw[hd][:, A_HEAD_DIM:], q[hd] * eg[hd]], axis=0).astype(BF16)
            attn_s[j] = (sc[hd][chunk:] * decay[hd]).astype(BF16)
            kg_s[j] = (k[hd] * jnp.exp(glast[hd] - gcol[hd])).astype(BF16)
            egl_s[j] = jnp.broadcast_to(jnp.exp(glast[hd]), (SUBLANES, LANES))
        return carry

    def scan_step(c, carry):
        r0 = pl.multiple_of(c * chunk, chunk)
        js = [c * A_HEADS + hd for hd in heads]
        s = [s_ref[hd] for hd in heads]
        ws = [_dot(wq_s[js[hd]], s[hd]) for hd in heads]
        v_new = [u_s[js[hd]] - ws[hd][:chunk] for hd in heads]
        o = [ws[hd][chunk:] + _dot(attn_s[js[hd]], v_new[hd]) for hd in heads]
        for hd in heads:
            s_ref[hd] = s[hd] * egl_s[js[hd]][0:1, :] + _dot_tn(kg_s[js[hd]], v_new[hd])
            o_s[pl.ds(r0, chunk), hd * A_HEAD_DIM:(hd + 1) * A_HEAD_DIM] = o[hd]
        return carry

    lax.fori_loop(0, tt // chunk, prep_step, 0)
    lax.fori_loop(0, tt // chunk, scan_step, 0)

    z = pa[:, 3 * A_WIDTH:4 * A_WIDTH]
    for hd in range(A_HEADS):
        lo = hd * A_HEAD_DIM
        o = o_s[:, lo:lo + A_HEAD_DIM]
        o = o * lax.rsqrt(jnp.mean(o * o, axis=-1, keepdims=True) + L2_EPS) * ng_ref[...]
        oa_ref[:, lo:lo + A_HEAD_DIM] = (o * _silu(z[:, lo:lo + A_HEAD_DIM])).astype(oa_ref.dtype)


def _gdn(x2d, norm1_g, w_a, conv_w, a_par, norm_g, *, batch, seq, tt, chunk):
    n, d = x2d.shape
    nt = seq // tt
    cols = w_a.shape[1]
    nj = tt // chunk * A_HEADS
    kern = functools.partial(_gdn_kernel, tt=tt, chunk=chunk)
    const = lambda b, t: (0, 0)
    return pl.pallas_call(
        kern,
        out_shape=jax.ShapeDtypeStruct((n, A_WIDTH), BF16),
        grid=(batch, nt),
        in_specs=[
            pl.BlockSpec((tt, d), lambda b, t: (b * nt + t, 0)),
            pl.BlockSpec((1, d), const),
            pl.BlockSpec((d, cols), const),
            pl.BlockSpec((A_CONV, 3 * A_WIDTH), const),
            pl.BlockSpec((SUBLANES, LANES), const),
            pl.BlockSpec((1, A_HEAD_DIM), const),
        ],
        out_specs=pl.BlockSpec((tt, A_WIDTH), lambda b, t: (b * nt + t, 0)),
        scratch_shapes=[
            pltpu.VMEM((tt + SUBLANES, 3 * A_WIDTH), F32),
            pltpu.VMEM((A_HEADS, A_HEAD_DIM, A_HEAD_DIM), F32),
            pltpu.VMEM((tt, A_WIDTH), F32),
            pltpu.VMEM((tt, A_WIDTH), F32),
            pltpu.VMEM((tt, A_WIDTH), F32),
            pltpu.VMEM((tt, A_WIDTH), F32),
            pltpu.VMEM((tt, LANES), F32),
            pltpu.VMEM((tt, LANES), F32),
            pltpu.VMEM((nj, chunk, A_HEAD_DIM), F32),
            pltpu.VMEM((nj, 2 * chunk, A_HEAD_DIM), BF16),
            pltpu.VMEM((nj, chunk, chunk), BF16),
            pltpu.VMEM((nj, chunk, A_HEAD_DIM), BF16),
            pltpu.VMEM((nj, SUBLANES, LANES), F32),
        ],
        compiler_params=pltpu.CompilerParams(
            dimension_semantics=("parallel", "arbitrary"), vmem_limit_bytes=VMEM_LIMIT),
        name="gdn",
    )(x2d, norm1_g, w_a, conv_w, a_par, norm_g)


def _rwkv_kernel(x_ref, g1_ref, wb_ref, mu_ref, wlora_ref, g2_ref, vec_ref, hsum_ref, ob_ref,
                 ext_ref, s_ref, ar_s, kb_s, hh_s, vs_s, av_s, arb_s, t_s, o_s, wtot_s, *, tt, chunk):
    bcols = 3 * B_WIDTH + B_DECAY_LORA + B_ICLR_LORA + B_GATE_LORA
    npairs = B_WIDTH // LANES

    @pl.when(pl.program_id(1) == 0)
    def _():
        ext_ref[0:SUBLANES, :] = jnp.zeros((SUBLANES, bcols), F32)
        s_ref[...] = jnp.zeros(s_ref.shape, F32)

    h = _rmsnorm(x_ref[...], g1_ref[...]).astype(BF16)
    pb = jnp.dot(h, wb_ref[...], preferred_element_type=F32)

    ext_ref[SUBLANES:SUBLANES + tt, :] = pb
    prev = ext_ref[SUBLANES - 1:SUBLANES - 1 + tt, :]
    ext_ref[0:SUBLANES, :] = ext_ref[tt:tt + SUBLANES, :]
    xm = pb + (prev - pb) * mu_ref[...]

    r = xm[:, 0:B_WIDTH]
    k = xm[:, B_WIDTH:2 * B_WIDTH]
    v = xm[:, 2 * B_WIDTH:3 * B_WIDTH]
    xwa = xm[:, 3 * B_WIDTH:3 * B_WIDTH + LANES]
    xg = xm[:, 3 * B_WIDTH + LANES:]
    lane = _iota2((tt, LANES), 1)
    lora_in = jnp.where(lane < B_DECAY_LORA, jnp.tanh(xwa), xwa)
    lora = _dot_x3(lora_in, wlora_ref[...])
    w0 = vec_ref[0:1, :]
    a0 = vec_ref[1:2, :]
    k_k = vec_ref[2:3, :]
    k_a = vec_ref[3:4, :]
    r_k = vec_ref[4:5, :]
    ln_w = vec_ref[5:6, :]
    ln_b = vec_ref[6:7, :]
    wl = -_softplus(-(w0 + lora[:, :B_WIDTH])) - 0.5
    lw = -jnp.exp(wl)
    a = _sigmoid(a0 + lora[:, B_WIDTH:])
    gate = _dot(_sigmoid(xg), g2_ref[...])

    hsum = hsum_ref[...]

    def head_sum(t):
        return jnp.dot(t.astype(BF16), hsum, preferred_element_type=F32)

    kk = k * k_k
    kk = kk * lax.rsqrt(head_sum(kk * kk) + L2_EPS)
    kf = k * (1.0 + (a - 1.0) * k_a)
    beta = kk * a

    lane2 = _iota2((chunk, LANES), 1)
    m0 = lane2 < B_HEAD_DIM

    def stack(*ts):
        parts = []
        for t in ts:
            parts += [jnp.where(m0, t, 0.0), jnp.where(m0, 0.0, t)]
        return jnp.concatenate(parts, axis=0).astype(BF16)

    tri = (_iota2((chunk, chunk), 1) <= _iota2((chunk, chunk), 0)).astype(F32)
    for c in range(tt // chunk):
        rs = slice(c * chunk, (c + 1) * chunk)
        lwc = lw[rs]
        gcum = _dot_exact_lhs(tri, lwc)
        gtot = gcum[chunk - 1:chunk, :]
        w_inv = jnp.exp(-gcum)
        w_rem = jnp.exp(gtot - gcum)
        at = kk[rs] * jnp.exp(gcum - lwc)
        kt = kf[rs] * w_inv
        bt = beta[rs] * w_inv
        rt = r[rs] * jnp.exp(gcum)
        kh = kf[rs] * w_rem
        bh = beta[rs] * w_rem
        vc = v[rs]
        wtot = jnp.broadcast_to(jnp.exp(gtot), (SUBLANES, B_WIDTH))
        for p in range(npairs):
            ls = slice(p * LANES, (p + 1) * LANES)
            j = c * npairs + p
            ar_s[j] = stack(at[:, ls], rt[:, ls])
            kb_s[j] = stack(kt[:, ls], bt[:, ls])
            hh_s[j] = stack(kh[:, ls], bh[:, ls])
            vs_s[j] = stack(vc[:, ls])
            wtot_s[j] = wtot[:, ls]

    n2 = 2 * chunk
    ri = _iota2((n2, n2), 0)
    ci = _iota2((n2, n2), 1)
    incl = ci <= ri
    strict = ci < ri
    pairs = range(npairs)

    def prep_step(c, carry):
        js = [c * npairs + p for p in pairs]
        sc = [_dot_nt(ar_s[j], kb_s[j]) for j in js]
        a_ab = [jnp.where(strict, s4[:n2, n2:], 0.0) for s4 in sc]
        tmat = _unit_lower_inverses(a_ab, chunk)
        for p in pairs:
            j = js[p]
            s4 = sc[p]
            a_kk = jnp.concatenate([jnp.where(strict, s4[:n2, :n2], 0.0),
                                    jnp.where(incl, s4[n2:, :n2], 0.0)], axis=0)
            av_s[j] = _dot(a_kk, vs_s[j])
            arb_s[j] = jnp.where(incl, s4[n2:, n2:], 0.0).astype(BF16)
            t_s[j] = tmat[p].astype(BF16)
        return carry

    def scan_step(c, carry):
        r0 = pl.multiple_of(c * chunk, chunk)
        js = [c * npairs + p for p in pairs]
        s = [s_ref[p] for p in pairs]
        ps = [_dot_nt(ar_s[js[p]], s[p]) + av_s[js[p]] for p in pairs]
        u = [_dot(t_s[js[p]], ps[p][:n2]) for p in pairs]
        o2 = [ps[p][n2:] - _dot(arb_s[js[p]], u[p]) for p in pairs]
        for p in pairs:
            vu = jnp.concatenate([vs_s[js[p]], (-u[p]).astype(BF16)], axis=0)
            s_ref[p] = s[p] * wtot_s[js[p]][0:1, :] + _dot_tn(vu, hh_s[js[p]])
            o_s[pl.ds(r0, chunk), p * LANES:(p + 1) * LANES] = o2[p][:chunk] + o2[p][chunk:]
        return carry

    lax.fori_loop(0, tt // chunk, prep_step, 0)
    lax.fori_loop(0, tt // chunk, scan_step, 0)

    out = o_s[...]
    mean = head_sum(out) * (1.0 / B_HEAD_DIM)
    cen = out - mean
    var = head_sum(cen * cen) * (1.0 / B_HEAD_DIM)
    out = cen * lax.rsqrt(var + B_GN_EPS) * ln_w + ln_b
    bonus = head_sum(r * kf * r_k) * v
    ob_ref[...] = ((out + bonus) * gate).astype(ob_ref.dtype)


def _rwkv(x2d, norm1_g, w_b, mu, w_lora, g2, vecs, hsum, *, batch, seq, tt, chunk):
    n, d = x2d.shape
    nt = seq // tt
    bcols = w_b.shape[1]
    kern = functools.partial(_rwkv_kernel, tt=tt, chunk=chunk)
    const = lambda b, t: (0, 0)
    nj = tt // chunk * (B_WIDTH // LANES)
    return pl.pallas_call(
        kern,
        out_shape=jax.ShapeDtypeStruct((n, B_WIDTH), BF16),
        grid=(batch, nt),
        in_specs=[
            pl.BlockSpec((tt, d), lambda b, t: (b * nt + t, 0)),
            pl.BlockSpec((1, d), const),
            pl.BlockSpec((d, bcols), const),
            pl.BlockSpec((1, bcols), const),
            pl.BlockSpec((LANES, 2 * B_WIDTH), const),
            pl.BlockSpec((B_GATE_LORA, B_WIDTH), const),
            pl.BlockSpec((SUBLANES, B_WIDTH), const),
            pl.BlockSpec((B_WIDTH, B_WIDTH), const),
        ],
        out_specs=pl.BlockSpec((tt, B_WIDTH), lambda b, t: (b * nt + t, 0)),
        scratch_shapes=[
            pltpu.VMEM((tt + SUBLANES, bcols), F32),
            pltpu.VMEM((B_WIDTH // LANES, LANES, LANES), F32),
            pltpu.VMEM((nj, 4 * chunk, LANES), BF16),
            pltpu.VMEM((nj, 4 * chunk, LANES), BF16),
            pltpu.VMEM((nj, 4 * chunk, LANES), BF16),
            pltpu.VMEM((nj, 2 * chunk, LANES), BF16),
            pltpu.VMEM((nj, 4 * chunk, LANES), F32),
            pltpu.VMEM((nj, 2 * chunk, 2 * chunk), BF16),
            pltpu.VMEM((nj, 2 * chunk, 2 * chunk), BF16),
            pltpu.VMEM((tt, B_WIDTH), F32),
            pltpu.VMEM((nj, SUBLANES, LANES), F32),
        ],
        compiler_params=pltpu.CompilerParams(
            dimension_semantics=("parallel", "arbitrary"), vmem_limit_bytes=VMEM_LIMIT),
        name="rwkv",
    )(x2d, norm1_g, w_b, mu, w_lora, g2, vecs, hsum)


def _merge_kernel(x_ref, g1_ref, oa_ref, ob_ref, wg_ref, wua_ref, wub_ref, wo_ref, x2_ref):
    d = x_ref.shape[1]
    x = x_ref[...]
    h = _rmsnorm(x, g1_ref[...]).astype(BF16)
    pg = jnp.dot(h, wg_ref[...], preferred_element_type=F32)
    ya = jnp.dot(oa_ref[...], wua_ref[...], preferred_element_type=F32)
    yb = jnp.dot(ob_ref[...], wub_ref[...], preferred_element_type=F32)
    merged = _sigmoid(pg[:, :d]) * ya + _sigmoid(pg[:, d:]) * yb
    x2_ref[...] = x + jnp.dot(merged.astype(BF16), wo_ref[...], preferred_element_type=F32)


def _merge(x2d, norm1_g, oa, ob, w_g, w_up_a, w_up_b, w_out, *, tm):
    n, d = x2d.shape
    const = lambda i: (0, 0)
    rows = lambda i: (i, 0)
    return pl.pallas_call(
        _merge_kernel,
        out_shape=jax.ShapeDtypeStruct((n, d), F32),
        grid=(n // tm,),
        in_specs=[
            pl.BlockSpec((tm, d), rows),
            pl.BlockSpec((1, d), const),
            pl.BlockSpec((tm, A_WIDTH), rows),
            pl.BlockSpec((tm, B_WIDTH), rows),
            pl.BlockSpec((d, 2 * d), const),
            pl.BlockSpec((A_WIDTH, d), const),
            pl.BlockSpec((B_WIDTH, d), const),
            pl.BlockSpec((d, d), const),
        ],
        out_specs=pl.BlockSpec((tm, d), rows),
        compiler_params=pltpu.CompilerParams(
            dimension_semantics=("parallel",), vmem_limit_bytes=VMEM_LIMIT),
        name="merge",
    )(x2d, norm1_g, oa, ob, w_g, w_up_a, w_up_b, w_out)


def _router_kernel(x2_ref, g2_ref, wr_ref, br_ref, h2_ref, meta_ref, cnt_ref, run_ref):
    tm = x2_ref.shape[0]

    @pl.when(pl.program_id(0) == 0)
    def _():
        run_ref[...] = jnp.zeros(run_ref.shape, F32)

    h2 = _rmsnorm(x2_ref[...], g2_ref[...])
    for c in range(h2_ref.shape[1]):
        h2_ref[:, c, :] = h2[:, c * LANES:(c + 1) * LANES]
    logits = _dot_f32(h2, wr_ref[...]) + br_ref[...]
    lane = _iota2((tm, LANES), 1)
    neg = jnp.float32(-jnp.inf)

    work = logits
    sel_idx = []
    sel_val = []
    multi = jnp.zeros((tm, LANES), F32)
    for _ in range(TOP_K):
        mx = jnp.max(work, axis=-1, keepdims=True)
        idx = jnp.min(jnp.where(work == mx, lane, LANES), axis=-1, keepdims=True)
        hit = lane == idx
        multi = multi + hit.astype(F32)
        work = jnp.where(hit, neg, work)
        sel_idx.append(idx)
        sel_val.append(mx)
    ex = [jnp.exp(vv - sel_val[0]) for vv in sel_val]
    den = ex[0] + ex[1] + ex[2] + ex[3]
    gates = [e / den for e in ex]

    run_ref[0:1, :] = run_ref[0:1, :] + jnp.sum(multi, axis=0, keepdims=True)
    cnt_ref[...] = run_ref[...]

    meta = jnp.zeros((tm, LANES), F32)
    for kk in range(TOP_K):
        meta = jnp.where(lane == kk, sel_idx[kk].astype(F32), meta)
        meta = jnp.where(lane == TOP_K + kk, gates[kk], meta)
    meta_ref[...] = meta


def _router(x2, norm2_g, w_r, b_r, *, tm):
    n, d = x2.shape
    const = lambda i: (0, 0)
    rows = lambda i: (i, 0)
    return pl.pallas_call(
        _router_kernel,
        out_shape=(jax.ShapeDtypeStruct((n, d // LANES, LANES), F32),
                   jax.ShapeDtypeStruct((n, LANES), F32),
                   jax.ShapeDtypeStruct((SUBLANES, LANES), F32)),
        grid=(n // tm,),
        in_specs=[
            pl.BlockSpec((tm, d), rows),
            pl.BlockSpec((1, d), const),
            pl.BlockSpec((d, LANES), const),
            pl.BlockSpec((1, LANES), const),
        ],
        out_specs=(pl.BlockSpec((tm, d // LANES, LANES), lambda i: (i, 0, 0)),
                   pl.BlockSpec((tm, LANES), rows),
                   pl.BlockSpec((SUBLANES, LANES), const)),
        scratch_shapes=[pltpu.VMEM((SUBLANES, LANES), F32)],
        compiler_params=pltpu.CompilerParams(
            dimension_semantics=("arbitrary",), vmem_limit_bytes=VMEM_LIMIT),
        name="router",
    )(x2, norm2_g, w_r, b_r)


def _experts_kernel(te_ref, nu_ref, cur_ref, nxt_ref, h2_ref, w1_ref, b1_ref, w2_ref, b2_ref, y4_ref,
                    xbuf, ybuf, gsem, ssem):
    del te_ref
    i = pl.program_id(0)
    tm = xbuf.shape[1]
    nc = xbuf.shape[2]
    de = w2_ref.shape[1]
    n_tok = h2_ref.shape[0]
    n_used = nu_ref[0]
    unroll = 8

    def token_of(dst_row):
        return dst_row & (n_tok - 1) if n_tok & (n_tok - 1) == 0 else lax.rem(dst_row, n_tok)

    def gather_row(idx_ref, slot, r):
        return pltpu.make_async_copy(h2_ref.at[token_of(idx_ref[0, 0, r])], xbuf.at[slot, r], gsem.at[slot])

    def gather_wait_all(slot):
        for r in range(tm):
            pltpu.make_async_copy(h2_ref.at[0], xbuf.at[slot, r], gsem.at[slot]).wait()

    def scatter_row(slot, r):
        return pltpu.make_async_copy(ybuf.at[slot, r], y4_ref.at[cur_ref[0, 0, r]], ssem.at[slot])

    def scatter_wait_all(slot):
        for r in range(tm):
            pltpu.make_async_copy(ybuf.at[slot, r], y4_ref.at[0], ssem.at[slot]).wait()

    def rows_loop(fn):
        def body(g, carry):
            for u in range(unroll):
                fn(g * unroll + u, u)
            return carry
        lax.fori_loop(0, tm // unroll, body, 0)

    @pl.when(i == 0)
    def _():
        rows_loop(lambda r, u: gather_row(cur_ref, 0, r).start(priority=u % DMA_QUEUES))

    @pl.when(jnp.logical_and(i >= 2, i < n_used))
    def _():
        scatter_wait_all(i % 2)

    @pl.when(i < n_used)
    def _():
        slot = i % 2
        gather_wait_all(slot)
        x = jnp.concatenate([xbuf[slot, :, c, :] for c in range(nc)], axis=1).astype(BF16)
        nblk = 4
        bw = de // nblk
        acts = []
        for j in range(nblk):
            for r in range(j * tm // nblk, (j + 1) * tm // nblk):
                gather_row(nxt_ref, 1 - slot, r).start(priority=r % DMA_QUEUES)
            cg = slice(j * bw, (j + 1) * bw)
            cl = slice(de + j * bw, de + (j + 1) * bw)
            hg = jnp.dot(x, w1_ref[0, :, cg].astype(BF16), preferred_element_type=F32) + b1_ref[0, :, cg]
            hl = jnp.dot(x, w1_ref[0, :, cl].astype(BF16), preferred_element_type=F32) + b1_ref[0, :, cl]
            glu = jnp.minimum(hg, SWIGLU_LIMIT)
            lin = jnp.clip(hl, -SWIGLU_LIMIT, SWIGLU_LIMIT)
            acts.append((glu * _sigmoid(SWIGLU_ALPHA * glu) * (lin + 1.0)).astype(BF16))
        act = jnp.concatenate(acts, axis=1)
        y = jnp.dot(act, w2_ref[0].astype(BF16), preferred_element_type=F32) + b2_ref[0]
        for c in range(nc):
            ybuf[slot, :, c, :] = y[:, c * LANES:(c + 1) * LANES]
        rows_loop(lambda r, u: scatter_row(slot, r).start(priority=u % DMA_QUEUES))

    @pl.when(i == n_used - 1)
    def _():
        scatter_wait_all(i % 2)
        gather_wait_all(1 - i % 2)

    @pl.when(jnp.logical_and(i == n_used - 1, i >= 1))
    def _():
        scatter_wait_all(1 - i % 2)


def _experts(tile_expert, n_used, dst, h2, w1, b1, w2, b2, *, tm):
    n, nc, _ = h2.shape
    d = nc * LANES
    n_tiles = dst.shape[0]
    de2 = w1.shape[2]
    de = w2.shape[1]
    wsel = lambda i, te, nu: (te[i], 0, 0)
    cur = lambda i, te, nu: (jnp.minimum(i, nu[0] - 1), 0, 0)
    nxt = lambda i, te, nu: (jnp.minimum(i + 1, nu[0] - 1), 0, 0)
    return pl.pallas_call(
        _experts_kernel,
        out_shape=jax.ShapeDtypeStruct((n_tiles * tm, nc, LANES), F32),
        grid_spec=pltpu.PrefetchScalarGridSpec(
            num_scalar_prefetch=2,
            grid=(n_tiles,),
            in_specs=[pl.BlockSpec((1, 1, tm), cur, memory_space=pltpu.SMEM),
                      pl.BlockSpec((1, 1, tm), nxt, memory_space=pltpu.SMEM),
                      pl.BlockSpec(memory_space=pl.ANY),
                      pl.BlockSpec((1, d, de2), wsel),
                      pl.BlockSpec((1, 1, de2), wsel),
                      pl.BlockSpec((1, de, d), wsel),
                      pl.BlockSpec((1, 1, d), wsel)],
            out_specs=pl.BlockSpec(memory_space=pl.ANY),
            scratch_shapes=[pltpu.VMEM((2, tm, nc, LANES), F32),
                            pltpu.VMEM((2, tm, nc, LANES), F32),
                            pltpu.SemaphoreType.DMA((2,)),
                            pltpu.SemaphoreType.DMA((2,))],
        ),
        compiler_params=pltpu.CompilerParams(
            dimension_semantics=("arbitrary",), vmem_limit_bytes=VMEM_LIMIT, has_side_effects=True),
        name="experts",
    )(tile_expert, n_used, dst, dst, h2, w1, b1, w2, b2)


def _combine_kernel(x2_ref, meta_ref, fg_ref, *refs, final_norm):
    y_refs, out_ref = refs[:TOP_K], refs[TOP_K]
    nc = y_refs[0].shape[1]
    meta = meta_ref[...]
    gates = [meta[:, TOP_K + kk:TOP_K + kk + 1] for kk in range(TOP_K)]
    cols = []
    for c in range(nc):
        acc = x2_ref[:, c * LANES:(c + 1) * LANES]
        for kk in range(TOP_K):
            acc = acc + y_refs[kk][:, c, :] * gates[kk]
        cols.append(acc)
    acc = jnp.concatenate(cols, axis=1)
    out_ref[...] = _rmsnorm(acc, fg_ref[...]) if final_norm else acc


def _combine(x2, meta, final_g, y4, *, tm, final_norm):
    n, d = x2.shape
    nc = y4.shape[1]
    nb = n // tm
    rows = lambda i: (i, 0)
    slot_rows = lambda kk: (lambda i: (kk * nb + i, 0, 0))
    return pl.pallas_call(
        functools.partial(_combine_kernel, final_norm=final_norm),
        out_shape=jax.ShapeDtypeStruct((n, d), F32),
        grid=(nb,),
        in_specs=[pl.BlockSpec((tm, d), rows),
                  pl.BlockSpec((tm, LANES), rows),
                  pl.BlockSpec((1, d), lambda i: (0, 0))]
                 + [pl.BlockSpec((tm, nc, LANES), slot_rows(kk)) for kk in range(TOP_K)],
        out_specs=pl.BlockSpec((tm, d), rows),
        compiler_params=pltpu.CompilerParams(
            dimension_semantics=("parallel",), vmem_limit_bytes=VMEM_LIMIT),
        name="combine",
    )(x2, meta, final_g, *([y4] * TOP_K))


def _pad_cols(w, cols):
    return jnp.pad(w, ((0, 0), (0, cols - w.shape[1])))


def _layer(x2d, p, *, batch, seq, tiles):
    d = x2d.shape[1]
    a_cols = 4 * A_WIDTH + 2 * A_HEADS
    b_cols = 3 * B_WIDTH + B_DECAY_LORA + B_ICLR_LORA + B_GATE_LORA
    w_in = p["w_in"]
    w_a = _pad_cols(w_in[:, :a_cols], 4 * A_WIDTH + LANES).astype(BF16)
    w_b = w_in[:, a_cols:a_cols + b_cols].astype(BF16)
    w_g = w_in[:, a_cols + b_cols:].astype(BF16)
    g1 = p["norm1_g"].reshape(1, d)

    a_par = jnp.zeros((SUBLANES, LANES), F32)
    a_par = a_par.at[0, A_HEADS:2 * A_HEADS].set(-jnp.exp(p["a_A_log"]))
    a_par = a_par.at[1, A_HEADS:2 * A_HEADS].set(p["a_dt_bias"])
    oa = _gdn(x2d, g1, w_a, p["a_conv_w"], a_par, p["a_norm_g"].reshape(1, A_HEAD_DIM),
              batch=batch, seq=seq, tt=tiles["tt_a"], chunk=A_CHUNK)

    w_lora = jnp.zeros((LANES, 2 * B_WIDTH), F32)
    w_lora = w_lora.at[:B_DECAY_LORA, :B_WIDTH].set(p["b_w2"])
    w_lora = w_lora.at[B_DECAY_LORA:, B_WIDTH:].set(p["b_a2"])
    vecs = jnp.stack([p["b_w0"], p["b_a0"], p["b_k_k"], p["b_k_a"], p["b_r_k"].reshape(-1),
                      p["b_ln_w"], p["b_ln_b"], jnp.zeros((B_WIDTH,), F32)])
    hd = jnp.arange(B_WIDTH) // B_HEAD_DIM
    hsum = (hd[:, None] == hd[None, :]).astype(BF16)
    ob = _rwkv(x2d, g1, w_b, p["b_mu"].reshape(1, b_cols), w_lora, p["b_g2"], vecs, hsum,
               batch=batch, seq=seq, tt=tiles["tt_b"], chunk=B_CHUNK)

    x2 = _merge(x2d, g1, oa, ob, w_g, p["w_up_a"].astype(BF16), p["w_up_b"].astype(BF16),
                p["w_out"].astype(BF16), tm=tiles["tm_merge"])

    w_r = _pad_cols(p["w_router"], LANES)
    b_r = jnp.full((1, LANES), -jnp.inf, F32).at[0, :N_EXPERTS].set(p["b_router"])
    h2, meta, cnt = _router(x2, p["norm2_g"].reshape(1, d), w_r, b_r, tm=tiles["tm_router"])

    tm_e = tiles["tm_expert"]
    n = x2d.shape[0]
    nk = n * TOP_K
    counts = cnt[0, :N_EXPERTS].astype(I32)
    tiles_per = (counts + tm_e - 1) // tm_e
    tile_end = jnp.cumsum(tiles_per)
    pad_off = (tile_end - tiles_per) * tm_e
    off = jnp.cumsum(counts) - counts
    n_tiles = nk // tm_e + N_EXPERTS
    n_used = tile_end[-1:]
    tile_ids = jnp.arange(n_tiles, dtype=I32)
    tile_expert = jnp.minimum(jnp.sum(tile_ids[:, None] >= tile_end[None, :], axis=1), N_EXPERTS - 1).astype(I32)
    last_e = tile_expert[jnp.maximum(n_used[0] - 1, 0)]
    tile_expert = jnp.where(tile_ids < n_used[0], tile_expert, last_e)
    order = jnp.argsort(meta[:, :TOP_K].astype(I32).reshape(-1), stable=True).astype(I32)
    rows = jnp.arange(n_tiles * tm_e, dtype=I32)
    row_e = jnp.repeat(tile_expert, tm_e)
    local = rows - pad_off[row_e]
    valid = (local < counts[row_e]) & (rows < n_used[0] * tm_e)
    flat = order[jnp.clip(off[row_e] + local, 0, nk - 1)]
    pad_rank = jnp.cumsum(jnp.logical_not(valid).astype(I32)) - 1
    dst = jnp.where(valid, (flat % TOP_K) * n + flat // TOP_K, nk + pad_rank).reshape(n_tiles, 1, tm_e)

    y4 = _experts(tile_expert, n_used.astype(I32), dst, h2, p["w1"], p["b1"].reshape(N_EXPERTS, 1, -1),
                  p["w2"], p["b2"].reshape(N_EXPERTS, 1, -1), tm=tm_e)
    return x2, meta, y4


def _tiles(seq, n):
    return dict(tt_a=min(seq, 512), tt_b=min(seq, 512), tm_merge=min(n, 512), tm_router=min(n, 512),
                tm_expert=min(n, 512), tm_combine=min(n, 256))


def kernel(x, norm1_g, w_in, a_conv_w, a_A_log, a_dt_bias, a_norm_g, b_mu, b_w0, b_w2, b_a0, b_a2, b_g2, b_k_k, b_k_a, b_r_k, b_ln_w, b_ln_b, w_up_a, w_up_b, w_out, norm2_g, w_router, b_router, w1, b1, w2, b2, final_g):
    batch, seq, d = x.shape
    n = batch * seq
    depth = norm1_g.shape[0]
    tiles = _tiles(seq, n)
    params = dict(norm1_g=norm1_g, w_in=w_in, a_conv_w=a_conv_w, a_A_log=a_A_log, a_dt_bias=a_dt_bias,
                  a_norm_g=a_norm_g, b_mu=b_mu, b_w0=b_w0, b_w2=b_w2, b_a0=b_a0, b_a2=b_a2, b_g2=b_g2,
                  b_k_k=b_k_k, b_k_a=b_k_a, b_r_k=b_r_k, b_ln_w=b_ln_w, b_ln_b=b_ln_b, w_up_a=w_up_a,
                  w_up_b=w_up_b, w_out=w_out, norm2_g=norm2_g, w_router=w_router, b_router=b_router,
                  w1=w1, b1=b1, w2=w2, b2=b2)
    x2d = x.reshape(n, d)
    for l in range(depth):
        p = {k: v[l] for k, v in params.items()}
        x2, meta, y4 = _layer(x2d, p, batch=batch, seq=seq, tiles=tiles)
        x2d = _combine(x2, meta, final_g.reshape(1, d), y4, tm=tiles["tm_combine"],
                       final_norm=(l == depth - 1))
    return x2d.reshape(batch, seq, d)
```

```python
import functools

import jax
import jax.numpy as jnp
from jax import lax
from jax.experimental import pallas as pl
from jax.experimental.pallas import tpu as pltpu

F32 = jnp.float32
BF16 = jnp.bfloat16
I32 = jnp.int32

A_HEADS = 4
A_HEAD_DIM = 128
A_WIDTH = A_HEADS * A_HEAD_DIM
A_CONV = 4
B_HEADS = 8
B_HEAD_DIM = 64
B_WIDTH = B_HEADS * B_HEAD_DIM
B_DECAY_LORA = 64
B_ICLR_LORA = 64
B_GATE_LORA = 128
B_GN_EPS = 64e-5
N_EXPERTS = 32
TOP_K = 4
SWIGLU_ALPHA = 1.702
SWIGLU_LIMIT = 7.0
RMS_EPS = 1e-5
L2_EPS = 1e-6

LANES = 128
SUBLANES = 8
INV_BLOCK = 16
A_CHUNK = 128
B_CHUNK = 64
VMEM_LIMIT = 56 * 1024 * 1024
DMA_QUEUES = 2


def _dot(a, b):
    return jnp.dot(a.astype(BF16), b.astype(BF16), preferred_element_type=F32)


def _dot_nt(a, b):
    return lax.dot_general(a.astype(BF16), b.astype(BF16), (((1,), (1,)), ((), ())),
                           preferred_element_type=F32)


def _dot_tn(a, b):
    return lax.dot_general(a.astype(BF16), b.astype(BF16), (((0,), (0,)), ((), ())),
                           preferred_element_type=F32)


def _dot_f32(a, b):
    return jnp.dot(a, b, preferred_element_type=F32, precision=lax.Precision.HIGHEST)


def _split_bf16(a, terms):
    out = []
    for _ in range(terms):
        hi = a.astype(BF16)
        out.append(hi)
        a = a - hi.astype(F32)
    return out


def _dot_x3(a, b):
    ah, al = _split_bf16(a, 2)
    bh, bl = _split_bf16(b, 2)
    mm = lambda u, v: jnp.dot(u, v, preferred_element_type=F32)
    return mm(ah, bh) + mm(ah, bl) + mm(al, bh)


def _dot_exact_lhs(a01, b):
    a01 = a01.astype(BF16)
    return sum(jnp.dot(a01, t, preferred_element_type=F32) for t in _split_bf16(b, 3))


def _sigmoid(x):
    return 1.0 / (1.0 + jnp.exp(-x))


def _silu(x):
    return x * _sigmoid(x)


def _softplus(x):
    return jnp.maximum(x, 0.0) + jnp.log(1.0 + jnp.exp(-jnp.abs(x)))


def _rmsnorm(x, g):
    return x * lax.rsqrt(jnp.mean(x * x, axis=-1, keepdims=True) + RMS_EPS) * g


def _iota2(shape, dim):
    return lax.broadcasted_iota(I32, shape, dim)


def _unit_lower_inverses(mats, unit):
    n = mats[0].shape[0]
    row = _iota2((n, n), 0)
    col = _iota2((n, n), 1)
    eye = (row == col).astype(F32)
    same_blk = (row // INV_BLOCK) == (col // INV_BLOCK)
    ps = [jnp.where(same_blk, -a, 0.0) for a in mats]
    dinv = [eye + x for x in ps]
    for _ in range(3):
        ps = [_dot(p, p) for p in ps]
        dinv = [dv + _dot(dv, p) for dv, p in zip(dinv, ps)]
    ps = [-_dot(dv, jnp.where(same_blk, 0.0, a)) for dv, a in zip(dinv, mats)]
    ts = [eye + y for y in ps]
    k = 2
    while k < unit // INV_BLOCK:
        ps = [_dot(p, p) for p in ps]
        ts = [t + _dot(t, p) for t, p in zip(ts, ps)]
        k *= 2
    return [_dot(t, dv) for t, dv in zip(ts, dinv)]


def _gdn_kernel(x_ref, g1_ref, wa_ref, convw_ref, apar_ref, ng_ref, oa_ref,
                ext_ref, s_ref, q_s, k_s, v_s, o_s, beta_s, gc_s, u_s, wq_s, attn_s, kg_s, egl_s, *, tt, chunk):
    qkv_w = 3 * A_WIDTH

    @pl.when(pl.program_id(1) == 0)
    def _():
        ext_ref[0:SUBLANES, :] = jnp.zeros((SUBLANES, qkv_w), F32)
        s_ref[...] = jnp.zeros(s_ref.shape, F32)

    h = _rmsnorm(x_ref[...], g1_ref[...]).astype(BF16)
    pa = jnp.dot(h, wa_ref[...], preferred_element_type=F32)

    ext_ref[SUBLANES:SUBLANES + tt, :] = pa[:, :qkv_w]
    conv = jnp.zeros((tt, qkv_w), F32)
    for j in range(A_CONV):
        off = SUBLANES - (A_CONV - 1) + j
        conv = conv + convw_ref[j:j + 1, :] * ext_ref[off:off + tt, :]
    ext_ref[0:SUBLANES, :] = ext_ref[tt:tt + SUBLANES, :]
    qkv = _silu(conv)

    for hd in range(A_HEADS):
        lo = hd * A_HEAD_DIM
        qh = qkv[:, lo:lo + A_HEAD_DIM]
        kh = qkv[:, A_WIDTH + lo:A_WIDTH + lo + A_HEAD_DIM]
        qn = qh * lax.rsqrt(jnp.sum(qh * qh, axis=-1, keepdims=True) + L2_EPS) * (A_HEAD_DIM ** -0.5)
        kn = kh * lax.rsqrt(jnp.sum(kh * kh, axis=-1, keepdims=True) + L2_EPS)
        q_s[:, lo:lo + A_HEAD_DIM] = qn
        k_s[:, lo:lo + A_HEAD_DIM] = kn
    v_s[...] = qkv[:, 2 * A_WIDTH:]

    bg = pa[:, 4 * A_WIDTH:]
    beta_s[...] = _sigmoid(bg)
    g_all = apar_ref[0:1, :] * _softplus(bg + apar_ref[1:2, :])
    ri = _iota2((chunk, chunk), 0)
    ci = _iota2((chunk, chunk), 1)
    causal = ci <= ri
    strict = ci < ri
    tri = causal.astype(F32)
    for c in range(tt // chunk):
        gc_s[c * chunk:(c + 1) * chunk, :] = _dot_exact_lhs(tri, g_all[c * chunk:(c + 1) * chunk, :])

    heads = range(A_HEADS)

    def prep_step(c, carry):
        r0 = pl.multiple_of(c * chunk, chunk)
        rows = pl.ds(r0, chunk)
        gcc = gc_s[rows, :]
        gct = gcc.T
        bc = beta_s[rows, :]
        cols = [slice(hd * A_HEAD_DIM, (hd + 1) * A_HEAD_DIM) for hd in heads]
        q = [q_s[rows, cs] for cs in cols]
        k = [k_s[rows, cs] for cs in cols]
        v = [v_s[rows, cs] for cs in cols]
        beta = [bc[:, hd:hd + 1] for hd in heads]
        gcol = [gcc[:, A_HEADS + hd:A_HEADS + hd + 1] for hd in heads]
        grow = [gct[A_HEADS + hd:A_HEADS + hd + 1, :] for hd in heads]
        glast = [gcc[chunk - 1:chunk, A_HEADS + hd:A_HEADS + hd + 1] for hd in heads]
        decay = [jnp.where(causal, jnp.exp(jnp.where(causal, gcol[hd] - grow[hd], 0.0)), 0.0) for hd in heads]
        eg = [jnp.exp(gcol[hd]) for hd in heads]
        kb = [k[hd] * beta[hd] for hd in heads]
        sc = [_dot_nt(jnp.concatenate([kb[hd], q[hd]], axis=0), k[hd]) for hd in heads]
        lmat = [jnp.where(strict, sc[hd][:chunk] * decay[hd], 0.0) for hd in heads]
        tmat = _unit_lower_inverses(lmat, chunk)
        uw = [_dot(tmat[hd], jnp.concatenate([v[hd] * beta[hd], kb[hd] * eg[hd]], axis=1)) for hd in heads]
        for hd in heads:
            j = c * A_HEADS + hd
            u_s[j] = uw[hd][:, :A_HEAD_DIM]
            wq_s[j] = jnp.concatenate([uw[hd][:, A_HEAD_DIM:], q[hd] * eg[hd]], axis=0).astype(BF16)
            attn_s[j] = (sc[hd][chunk:] * decay[hd]).astype(BF16)
            kg_s[j] = (k[hd] * jnp.exp(glast[hd] - gcol[hd])).astype(BF16)
            egl_s[j] = jnp.broadcast_to(jnp.exp(glast[hd]), (SUBLANES, LANES))
        return carry

    def scan_step(c, carry):
        r0 = pl.multiple_of(c * chunk, chunk)
        js = [c * A_HEADS + hd for hd in heads]
        s = [s_ref[hd] for hd in heads]
        ws = [_dot(wq_s[js[hd]], s[hd]) for hd in heads]
        v_new = [u_s[js[hd]] - ws[hd][:chunk] for hd in heads]
        o = [ws[hd][chunk:] + _dot(attn_s[js[hd]], v_new[hd]) for hd in heads]
        for hd in heads:
            s_ref[hd] = s[hd] * egl_s[js[hd]][0:1, :] + _dot_tn(kg_s[js[hd]], v_new[hd])
            o_s[pl.ds(r0, chunk), hd * A_HEAD_DIM:(hd + 1) * A_HEAD_DIM] = o[hd]
        return carry

    lax.fori_loop(0, tt // chunk, prep_step, 0)
    lax.fori_loop(0, tt // chunk, scan_step, 0)

    z = pa[:, 3 * A_WIDTH:4 * A_WIDTH]
    for hd in range(A_HEADS):
        lo = hd * A_HEAD_DIM
        o = o_s[:, lo:lo + A_HEAD_DIM]
        o = o * lax.rsqrt(jnp.mean(o * o, axis=-1, keepdims=True) + L2_EPS) * ng_ref[...]
        oa_ref[:, lo:lo + A_HEAD_DIM] = (o * _silu(z[:, lo:lo + A_HEAD_DIM])).astype(oa_ref.dtype)


def _gdn(x2d, norm1_g, w_a, conv_w, a_par, norm_g, *, batch, seq, tt, chunk):
    n, d = x2d.shape
    nt = seq // tt
    cols = w_a.shape[1]
    nj = tt // chunk * A_HEADS
    kern = functools.partial(_gdn_kernel, tt=tt, chunk=chunk)
    const = lambda b, t: (0, 0)
    return pl.pallas_call(
        kern,
        out_shape=jax.ShapeDtypeStruct((n, A_WIDTH), BF16),
        grid=(batch, nt),
        in_specs=[
            pl.BlockSpec((tt, d), lambda b, t: (b * nt + t, 0)),
            pl.BlockSpec((1, d), const),
            pl.BlockSpec((d, cols), const),
            pl.BlockSpec((A_CONV, 3 * A_WIDTH), const),
            pl.BlockSpec((SUBLANES, LANES), const),
            pl.BlockSpec((1, A_HEAD_DIM), const),
        ],
        out_specs=pl.BlockSpec((tt, A_WIDTH), lambda b, t: (b * nt + t, 0)),
        scratch_shapes=[
            pltpu.VMEM((tt + SUBLANES, 3 * A_WIDTH), F32),
            pltpu.VMEM((A_HEADS, A_HEAD_DIM, A_HEAD_DIM), F32),
            pltpu.VMEM((tt, A_WIDTH), F32),
            pltpu.VMEM((tt, A_WIDTH), F32),
            pltpu.VMEM((tt, A_WIDTH), F32),
            pltpu.VMEM((tt, A_WIDTH), F32),
            pltpu.VMEM((tt, LANES), F32),
            pltpu.VMEM((tt, LANES), F32),
            pltpu.VMEM((nj, chunk, A_HEAD_DIM), F32),
            pltpu.VMEM((nj, 2 * chunk, A_HEAD_DIM), BF16),
            pltpu.VMEM((nj, chunk, chunk), BF16),
            pltpu.VMEM((nj, chunk, A_HEAD_DIM), BF16),
            pltpu.VMEM((nj, SUBLANES, LANES), F32),
        ],
        compiler_params=pltpu.CompilerParams(
            dimension_semantics=("parallel", "arbitrary"), vmem_limit_bytes=VMEM_LIMIT),
        name="gdn",
    )(x2d, norm1_g, w_a, conv_w, a_par, norm_g)


def _rwkv_kernel(x_ref, g1_ref, wb_ref, mu_ref, wlora_ref, g2_ref, vec_ref, hsum_ref, ob_ref,
                 ext_ref, s_ref, ar_s, kb_s, hh_s, vs_s, av_s, arb_s, t_s, o_s, wtot_s, *, tt, chunk):
    bcols = 3 * B_WIDTH + B_DECAY_LORA + B_ICLR_LORA + B_GATE_LORA
    npairs = B_WIDTH // LANES

    @pl.when(pl.program_id(1) == 0)
    def _():
        ext_ref[0:SUBLANES, :] = jnp.zeros((SUBLANES, bcols), F32)
        s_ref[...] = jnp.zeros(s_ref.shape, F32)

    h = _rmsnorm(x_ref[...], g1_ref[...]).astype(BF16)
    pb = jnp.dot(h, wb_ref[...], preferred_element_type=F32)

    ext_ref[SUBLANES:SUBLANES + tt, :] = pb
    prev = ext_ref[SUBLANES - 1:SUBLANES - 1 + tt, :]
    ext_ref[0:SUBLANES, :] = ext_ref[tt:tt + SUBLANES, :]
    xm = pb + (prev - pb) * mu_ref[...]

    r = xm[:, 0:B_WIDTH]
    k = xm[:, B_WIDTH:2 * B_WIDTH]
    v = xm[:, 2 * B_WIDTH:3 * B_WIDTH]
    xwa = xm[:, 3 * B_WIDTH:3 * B_WIDTH + LANES]
    xg = xm[:, 3 * B_WIDTH + LANES:]
    lane = _iota2((tt, LANES), 1)
    lora_in = jnp.where(lane < B_DECAY_LORA, jnp.tanh(xwa), xwa)
    lora = _dot_x3(lora_in, wlora_ref[...])
    w0 = vec_ref[0:1, :]
    a0 = vec_ref[1:2, :]
    k_k = vec_ref[2:3, :]
    k_a = vec_ref[3:4, :]
    r_k = vec_ref[4:5, :]
    ln_w = vec_ref[5:6, :]
    ln_b = vec_ref[6:7, :]
    wl = -_softplus(-(w0 + lora[:, :B_WIDTH])) - 0.5
    lw = -jnp.exp(wl)
    a = _sigmoid(a0 + lora[:, B_WIDTH:])
    gate = _dot(_sigmoid(xg), g2_ref[...])

    hsum = hsum_ref[...]

    def head_sum(t):
        return jnp.dot(t.astype(BF16), hsum, preferred_element_type=F32)

    kk = k * k_k
    kk = kk * lax.rsqrt(head_sum(kk * kk) + L2_EPS)
    kf = k * (1.0 + (a - 1.0) * k_a)
    beta = kk * a

    lane2 = _iota2((chunk, LANES), 1)
    m0 = lane2 < B_HEAD_DIM

    def stack(*ts):
        parts = []
        for t in ts:
            parts += [jnp.where(m0, t, 0.0), jnp.where(m0, 0.0, t)]
        return jnp.concatenate(parts, axis=0).astype(BF16)

    tri = (_iota2((chunk, chunk), 1) <= _iota2((chunk, chunk), 0)).astype(F32)
    for c in range(tt // chunk):
        rs = slice(c * chunk, (c + 1) * chunk)
        lwc = lw[rs]
        gcum = _dot_exact_lhs(tri, lwc)
        gtot = gcum[chunk - 1:chunk, :]
        w_inv = jnp.exp(-gcum)
        w_rem = jnp.exp(gtot - gcum)
        at = kk[rs] * jnp.exp(gcum - lwc)
        kt = kf[rs] * w_inv
        bt = beta[rs] * w_inv
        rt = r[rs] * jnp.exp(gcum)
        kh = kf[rs] * w_rem
        bh = beta[rs] * w_rem
        vc = v[rs]
        wtot = jnp.broadcast_to(jnp.exp(gtot), (SUBLANES, B_WIDTH))
        for p in range(npairs):
            ls = slice(p * LANES, (p + 1) * LANES)
            j = c * npairs + p
            ar_s[j] = stack(at[:, ls], rt[:, ls])
            kb_s[j] = stack(kt[:, ls], bt[:, ls])
            hh_s[j] = stack(kh[:, ls], bh[:, ls])
            vs_s[j] = stack(vc[:, ls])
            wtot_s[j] = wtot[:, ls]

    n2 = 2 * chunk
    ri = _iota2((n2, n2), 0)
    ci = _iota2((n2, n2), 1)
    incl = ci <= ri
    strict = ci < ri
    pairs = range(npairs)

    def prep_step(c, carry):
        js = [c * npairs + p for p in pairs]
        sc = [_dot_nt(ar_s[j], kb_s[j]) for j in js]
        a_ab = [jnp.where(strict, s4[:n2, n2:], 0.0) for s4 in sc]
        tmat = _unit_lower_inverses(a_ab, chunk)
        for p in pairs:
            j = js[p]
            s4 = sc[p]
            a_kk = jnp.concatenate([jnp.where(strict, s4[:n2, :n2], 0.0),
                                    jnp.where(incl, s4[n2:, :n2], 0.0)], axis=0)
            av_s[j] = _dot(a_kk, vs_s[j])
            arb_s[j] = jnp.where(incl, s4[n2:, n2:], 0.0).astype(BF16)
            t_s[j] = tmat[p].astype(BF16)
        return carry

    def scan_step(c, carry):
        r0 = pl.multiple_of(c * chunk, chunk)
        js = [c * npairs + p for p in pairs]
        s = [s_ref[p] for p in pairs]
        ps = [_dot_nt(ar_s[js[p]], s[p]) + av_s[js[p]] for p in pairs]
        u = [_dot(t_s[js[p]], ps[p][:n2]) for p in pairs]
        o2 = [ps[p][n2:] - _dot(arb_s[js[p]], u[p]) for p in pairs]
        for p in pairs:
            vu = jnp.concatenate([vs_s[js[p]], (-u[p]).astype(BF16)], axis=0)
            s_ref[p] = s[p] * wtot_s[js[p]][0:1, :] + _dot_tn(vu, hh_s[js[p]])
            o_s[pl.ds(r0, chunk), p * LANES:(p + 1) * LANES] = o2[p][:chunk] + o2[p][chunk:]
        return carry

    lax.fori_loop(0, tt // chunk, prep_step, 0)
    lax.fori_loop(0, tt // chunk, scan_step, 0)

    out = o_s[...]
    mean = head_sum(out) * (1.0 / B_HEAD_DIM)
    cen = out - mean
    var = head_sum(cen * cen) * (1.0 / B_HEAD_DIM)
    out = cen * lax.rsqrt(var + B_GN_EPS) * ln_w + ln_b
    bonus = head_sum(r * kf * r_k) * v
    ob_ref[...] = ((out + bonus) * gate).astype(ob_ref.dtype)


def _rwkv(x2d, norm1_g, w_b, mu, w_lora, g2, vecs, hsum, *, batch, seq, tt, chunk):
    n, d = x2d.shape
    nt = seq // tt
    bcols = w_b.shape[1]
    kern = functools.partial(_rwkv_kernel, tt=tt, chunk=chunk)
    const = lambda b, t: (0, 0)
    nj = tt // chunk * (B_WIDTH // LANES)
    return pl.pallas_call(
        kern,
        out_shape=jax.ShapeDtypeStruct((n, B_WIDTH), BF16),
        grid=(batch, nt),
        in_specs=[
            pl.BlockSpec((tt, d), lambda b, t: (b * nt + t, 0)),
            pl.BlockSpec((1, d), const),
            pl.BlockSpec((d, bcols), const),
            pl.BlockSpec((1, bcols), const),
            pl.BlockSpec((LANES, 2 * B_WIDTH), const),
            pl.BlockSpec((B_GATE_LORA, B_WIDTH), const),
            pl.BlockSpec((SUBLANES, B_WIDTH), const),
            pl.BlockSpec((B_WIDTH, B_WIDTH), const),
        ],
        out_specs=pl.BlockSpec((tt, B_WIDTH), lambda b, t: (b * nt + t, 0)),
        scratch_shapes=[
            pltpu.VMEM((tt + SUBLANES, bcols), F32),
            pltpu.VMEM((B_WIDTH // LANES, LANES, LANES), F32),
            pltpu.VMEM((nj, 4 * chunk, LANES), BF16),
            pltpu.VMEM((nj, 4 * chunk, LANES), BF16),
            pltpu.VMEM((nj, 4 * chunk, LANES), BF16),
            pltpu.VMEM((nj, 2 * chunk, LANES), BF16),
            pltpu.VMEM((nj, 4 * chunk, LANES), F32),
            pltpu.VMEM((nj, 2 * chunk, 2 * chunk), BF16),
            pltpu.VMEM((nj, 2 * chunk, 2 * chunk), BF16),
            pltpu.VMEM((tt, B_WIDTH), F32),
            pltpu.VMEM((nj, SUBLANES, LANES), F32),
        ],
        compiler_params=pltpu.CompilerParams(
            dimension_semantics=("parallel", "arbitrary"), vmem_limit_bytes=VMEM_LIMIT),
        name="rwkv",
    )(x2d, norm1_g, w_b, mu, w_lora, g2, vecs, hsum)


def _merge_kernel(x_ref, g1_ref, oa_ref, ob_ref, wg_ref, wua_ref, wub_ref, wo_ref, x2_ref):
    d = x_ref.shape[1]
    x = x_ref[...]
    h = _rmsnorm(x, g1_ref[...]).astype(BF16)
    pg = jnp.dot(h, wg_ref[...], preferred_element_type=F32)
    ya = jnp.dot(oa_ref[...], wua_ref[...], preferred_element_type=F32)
    yb = jnp.dot(ob_ref[...], wub_ref[...], preferred_element_type=F32)
    merged = _sigmoid(pg[:, :d]) * ya + _sigmoid(pg[:, d:]) * yb
    x2_ref[...] = x + jnp.dot(merged.astype(BF16), wo_ref[...], preferred_element_type=F32)


def _merge(x2d, norm1_g, oa, ob, w_g, w_up_a, w_up_b, w_out, *, tm):
    n, d = x2d.shape
    const = lambda i: (0, 0)
    rows = lambda i: (i, 0)
    return pl.pallas_call(
        _merge_kernel,
        out_shape=jax.ShapeDtypeStruct((n, d), F32),
        grid=(n // tm,),
        in_specs=[
            pl.BlockSpec((tm, d), rows),
            pl.BlockSpec((1, d), const),
            pl.BlockSpec((tm, A_WIDTH), rows),
            pl.BlockSpec((tm, B_WIDTH), rows),
            pl.BlockSpec((d, 2 * d), const),
            pl.BlockSpec((A_WIDTH, d), const),
            pl.BlockSpec((B_WIDTH, d), const),
            pl.BlockSpec((d, d), const),
        ],
        out_specs=pl.BlockSpec((tm, d), rows),
        compiler_params=pltpu.CompilerParams(
            dimension_semantics=("parallel",), vmem_limit_bytes=VMEM_LIMIT),
        name="merge",
    )(x2d, norm1_g, oa, ob, w_g, w_up_a, w_up_b, w_out)


def _router_kernel(x2_ref, g2_ref, wr_ref, br_ref, h2_ref, meta_ref, cnt_ref, run_ref):
    tm = x2_ref.shape[0]

    @pl.when(pl.program_id(0) == 0)
    def _():
        run_ref[...] = jnp.zeros(run_ref.shape, F32)

    h2 = _rmsnorm(x2_ref[...], g2_ref[...])
    h2_ref[...] = h2
    logits = _dot_f32(h2, wr_ref[...]) + br_ref[...]
    lane = _iota2((tm, LANES), 1)
    neg = jnp.float32(-jnp.inf)

    work = logits
    sel_idx = []
    sel_val = []
    multi = jnp.zeros((tm, LANES), F32)
    for _ in range(TOP_K):
        mx = jnp.max(work, axis=-1, keepdims=True)
        idx = jnp.min(jnp.where(work == mx, lane, LANES), axis=-1, keepdims=True)
        hit = lane == idx
        multi = multi + hit.astype(F32)
        work = jnp.where(hit, neg, work)
        sel_idx.append(idx)
        sel_val.append(mx)
    ex = [jnp.exp(vv - sel_val[0]) for vv in sel_val]
    den = ex[0] + ex[1] + ex[2] + ex[3]
    gates = [e / den for e in ex]

    run_ref[0:1, :] = run_ref[0:1, :] + jnp.sum(multi, axis=0, keepdims=True)
    cnt_ref[...] = run_ref[...]

    meta = jnp.zeros((tm, LANES), F32)
    for kk in range(TOP_K):
        meta = jnp.where(lane == kk, sel_idx[kk].astype(F32), meta)
        meta = jnp.where(lane == TOP_K + kk, gates[kk], meta)
    meta_ref[...] = meta


def _router(x2, norm2_g, w_r, b_r, *, tm):
    n, d = x2.shape
    const = lambda i: (0, 0)
    rows = lambda i: (i, 0)
    return pl.pallas_call(
        _router_kernel,
        out_shape=(jax.ShapeDtypeStruct((n, d), F32),
                   jax.ShapeDtypeStruct((n, LANES), F32),
                   jax.ShapeDtypeStruct((SUBLANES, LANES), F32)),
        grid=(n // tm,),
        in_specs=[
            pl.BlockSpec((tm, d), rows),
            pl.BlockSpec((1, d), const),
            pl.BlockSpec((d, LANES), const),
            pl.BlockSpec((1, LANES), const),
        ],
        out_specs=(pl.BlockSpec((tm, d), rows),
                   pl.BlockSpec((tm, LANES), rows),
                   pl.BlockSpec((SUBLANES, LANES), const)),
        scratch_shapes=[pltpu.VMEM((SUBLANES, LANES), F32)],
        compiler_params=pltpu.CompilerParams(
            dimension_semantics=("arbitrary",), vmem_limit_bytes=VMEM_LIMIT),
        name="router",
    )(x2, norm2_g, w_r, b_r)


def _experts_kernel(te_ref, nu_ref, cur_ref, nxt_ref, prv_ref, h2_ref, w1_ref, b1_ref, w2_ref, b2_ref, y4_ref,
                    xbuf, ybuf, gsem, ssem):
    del te_ref
    i = pl.program_id(0)
    tm = xbuf.shape[1]
    d = xbuf.shape[2]
    de = w2_ref.shape[1]
    n_tok = h2_ref.shape[0]
    n_used = nu_ref[0]
    first_spare = y4_ref.shape[0] - tm
    unroll = 8
    nblk = 4

    def token_of(dst_row):
        return dst_row & (n_tok - 1) if n_tok & (n_tok - 1) == 0 else lax.rem(dst_row, n_tok)

    def gather_row(idx_ref, slot, r):
        return pltpu.make_async_copy(h2_ref.at[pl.ds(token_of(idx_ref[0, 0, r]), 1), :],
                                     xbuf.at[slot, pl.ds(r, 1), :], gsem.at[slot])

    def scatter_row(dst_row, slot, r):
        return pltpu.make_async_copy(ybuf.at[slot, pl.ds(r, 1), :], y4_ref.at[pl.ds(dst_row, 1), :],
                                     ssem.at[slot])

    def gather_wait_all(slot):
        for r in range(tm):
            pltpu.make_async_copy(h2_ref.at[pl.ds(0, 1), :], xbuf.at[slot, pl.ds(r, 1), :], gsem.at[slot]).wait()

    def scatter_wait_all(slot):
        for r in range(tm):
            scatter_row(0, slot, r).wait()

    def rows_loop(fn):
        def body(g, carry):
            for u in range(unroll):
                fn(g * unroll + u, u)
            return carry
        lax.fori_loop(0, tm // unroll, body, 0)

    @pl.when(i == 0)
    def _():
        ybuf[...] = jnp.zeros(ybuf.shape, F32)
        rows_loop(lambda r, u: gather_row(cur_ref, 0, r).start(priority=u % DMA_QUEUES))
        rows_loop(lambda r, u: scatter_row(first_spare + r, 0, r).start(priority=u % DMA_QUEUES))

    @pl.when(i < n_used)
    def _():
        slot = i % 2
        gather_wait_all(slot)
        x = xbuf[slot].astype(BF16)
        bw = de // nblk
        acts = []
        for j in range(nblk):
            for r in range(j * tm // nblk, (j + 1) * tm // nblk):
                gather_row(nxt_ref, 1 - slot, r).start(priority=r % DMA_QUEUES)
            cg = slice(j * bw, (j + 1) * bw)
            cl = slice(de + j * bw, de + (j + 1) * bw)
            hg = jnp.dot(x, w1_ref[0, :, cg].astype(BF16), preferred_element_type=F32) + b1_ref[0, :, cg]
            hl = jnp.dot(x, w1_ref[0, :, cl].astype(BF16), preferred_element_type=F32) + b1_ref[0, :, cl]
            glu = jnp.minimum(hg, SWIGLU_LIMIT)
            lin = jnp.clip(hl, -SWIGLU_LIMIT, SWIGLU_LIMIT)
            acts.append((glu * _sigmoid(SWIGLU_ALPHA * glu) * (lin + 1.0)).astype(BF16))
        act = jnp.concatenate(acts, axis=1)
        bw = d // nblk
        ys = []
        for j in range(nblk):
            for r in range(j * tm // nblk, (j + 1) * tm // nblk):
                scatter_row(prv_ref[0, 0, r], 1 - slot, r).start(priority=r % DMA_QUEUES)
            cs = slice(j * bw, (j + 1) * bw)
            ys.append(jnp.dot(act, w2_ref[0, :, cs].astype(BF16), preferred_element_type=F32) + b2_ref[0, :, cs])
        scatter_wait_all(slot)
        ybuf[slot] = jnp.concatenate(ys, axis=1)

    @pl.when(i == n_used - 1)
    def _():
        slot = i % 2
        scatter_wait_all(1 - slot)
        rows_loop(lambda r, u: scatter_row(cur_ref[0, 0, r], slot, r).start(priority=u % DMA_QUEUES))
        scatter_wait_all(slot)
        gather_wait_all(1 - slot)


def _experts(tile_expert, n_used, dst, h2, w1, b1, w2, b2, *, tm):
    n, d = h2.shape
    n_tiles = dst.shape[0] - 1
    de2 = w1.shape[2]
    de = w2.shape[1]
    wsel = lambda i, te, nu: (te[i], 0, 0)
    cur = lambda i, te, nu: (jnp.minimum(i, nu[0] - 1), 0, 0)
    nxt = lambda i, te, nu: (jnp.minimum(i + 1, nu[0] - 1), 0, 0)
    prv = lambda i, te, nu: (jnp.where(i == 0, n_tiles, jnp.minimum(i, nu[0]) - 1), 0, 0)
    idx_spec = lambda fn: pl.BlockSpec((1, 1, tm), fn, memory_space=pltpu.SMEM)
    return pl.pallas_call(
        _experts_kernel,
        out_shape=jax.ShapeDtypeStruct(((n_tiles + 2) * tm, d), F32),
        grid_spec=pltpu.PrefetchScalarGridSpec(
            num_scalar_prefetch=2,
            grid=(n_tiles,),
            in_specs=[idx_spec(cur), idx_spec(nxt), idx_spec(prv),
                      pl.BlockSpec(memory_space=pl.ANY),
                      pl.BlockSpec((1, d, de2), wsel),
                      pl.BlockSpec((1, 1, de2), wsel),
                      pl.BlockSpec((1, de, d), wsel),
                      pl.BlockSpec((1, 1, d), wsel)],
            out_specs=pl.BlockSpec(memory_space=pl.ANY),
            scratch_shapes=[pltpu.VMEM((2, tm, d), F32),
                            pltpu.VMEM((2, tm, d), F32),
                            pltpu.SemaphoreType.DMA((2,)),
                            pltpu.SemaphoreType.DMA((2,))],
        ),
        compiler_params=pltpu.CompilerParams(
            dimension_semantics=("arbitrary",), vmem_limit_bytes=VMEM_LIMIT, has_side_effects=True),
        name="experts",
    )(tile_expert, n_used, dst, dst, dst, h2, w1, b1, w2, b2)


def _combine_kernel(x2_ref, meta_ref, fg_ref, *refs, final_norm):
    y_refs, out_ref = refs[:TOP_K], refs[TOP_K]
    meta = meta_ref[...]
    acc = x2_ref[...]
    for kk in range(TOP_K):
        acc = acc + y_refs[kk][...] * meta[:, TOP_K + kk:TOP_K + kk + 1]
    out_ref[...] = _rmsnorm(acc, fg_ref[...]) if final_norm else acc


def _combine(x2, meta, final_g, y4, *, tm, final_norm):
    n, d = x2.shape
    nb = n // tm
    rows = lambda i: (i, 0)
    slot_rows = lambda kk: (lambda i: (kk * nb + i, 0))
    return pl.pallas_call(
        functools.partial(_combine_kernel, final_norm=final_norm),
        out_shape=jax.ShapeDtypeStruct((n, d), F32),
        grid=(nb,),
        in_specs=[pl.BlockSpec((tm, d), rows),
                  pl.BlockSpec((tm, LANES), rows),
                  pl.BlockSpec((1, d), lambda i: (0, 0))]
                 + [pl.BlockSpec((tm, d), slot_rows(kk)) for kk in range(TOP_K)],
        out_specs=pl.BlockSpec((tm, d), rows),
        compiler_params=pltpu.CompilerParams(
            dimension_semantics=("parallel",), vmem_limit_bytes=VMEM_LIMIT),
        name="combine",
    )(x2, meta, final_g, *([y4] * TOP_K))


def _pad_cols(w, cols):
    return jnp.pad(w, ((0, 0), (0, cols - w.shape[1])))


def _layer(x2d, p, *, batch, seq, tiles):
    d = x2d.shape[1]
    a_cols = 4 * A_WIDTH + 2 * A_HEADS
    b_cols = 3 * B_WIDTH + B_DECAY_LORA + B_ICLR_LORA + B_GATE_LORA
    w_in = p["w_in"]
    w_a = _pad_cols(w_in[:, :a_cols], 4 * A_WIDTH + LANES).astype(BF16)
    w_b = w_in[:, a_cols:a_cols + b_cols].astype(BF16)
    w_g = w_in[:, a_cols + b_cols:].astype(BF16)
    g1 = p["norm1_g"].reshape(1, d)

    a_par = jnp.zeros((SUBLANES, LANES), F32)
    a_par = a_par.at[0, A_HEADS:2 * A_HEADS].set(-jnp.exp(p["a_A_log"]))
    a_par = a_par.at[1, A_HEADS:2 * A_HEADS].set(p["a_dt_bias"])
    oa = _gdn(x2d, g1, w_a, p["a_conv_w"], a_par, p["a_norm_g"].reshape(1, A_HEAD_DIM),
              batch=batch, seq=seq, tt=tiles["tt_a"], chunk=A_CHUNK)

    w_lora = jnp.zeros((LANES, 2 * B_WIDTH), F32)
    w_lora = w_lora.at[:B_DECAY_LORA, :B_WIDTH].set(p["b_w2"])
    w_lora = w_lora.at[B_DECAY_LORA:, B_WIDTH:].set(p["b_a2"])
    vecs = jnp.stack([p["b_w0"], p["b_a0"], p["b_k_k"], p["b_k_a"], p["b_r_k"].reshape(-1),
                      p["b_ln_w"], p["b_ln_b"], jnp.zeros((B_WIDTH,), F32)])
    hd = jnp.arange(B_WIDTH) // B_HEAD_DIM
    hsum = (hd[:, None] == hd[None, :]).astype(BF16)
    ob = _rwkv(x2d, g1, w_b, p["b_mu"].reshape(1, b_cols), w_lora, p["b_g2"], vecs, hsum,
               batch=batch, seq=seq, tt=tiles["tt_b"], chunk=B_CHUNK)

    x2 = _merge(x2d, g1, oa, ob, w_g, p["w_up_a"].astype(BF16), p["w_up_b"].astype(BF16),
                p["w_out"].astype(BF16), tm=tiles["tm_merge"])

    w_r = _pad_cols(p["w_router"], LANES)
    b_r = jnp.full((1, LANES), -jnp.inf, F32).at[0, :N_EXPERTS].set(p["b_router"])
    h2, meta, cnt = _router(x2, p["norm2_g"].reshape(1, d), w_r, b_r, tm=tiles["tm_router"])

    tm_e = tiles["tm_expert"]
    n = x2d.shape[0]
    nk = n * TOP_K
    counts = cnt[0, :N_EXPERTS].astype(I32)
    tiles_per = (counts + tm_e - 1) // tm_e
    tile_end = jnp.cumsum(tiles_per)
    pad_off = (tile_end - tiles_per) * tm_e
    off = jnp.cumsum(counts) - counts
    n_tiles = nk // tm_e + N_EXPERTS
    n_used = tile_end[-1:]
    tile_ids = jnp.arange(n_tiles, dtype=I32)
    tile_expert = jnp.minimum(jnp.sum(tile_ids[:, None] >= tile_end[None, :], axis=1), N_EXPERTS - 1).astype(I32)
    last_e = tile_expert[jnp.maximum(n_used[0] - 1, 0)]
    tile_expert = jnp.where(tile_ids < n_used[0], tile_expert, last_e)
    order = jnp.argsort(meta[:, :TOP_K].astype(I32).reshape(-1), stable=True).astype(I32)
    rows = jnp.arange(n_tiles * tm_e, dtype=I32)
    row_e = jnp.repeat(tile_expert, tm_e)
    local = rows - pad_off[row_e]
    valid = (local < counts[row_e]) & (rows < n_used[0] * tm_e)
    flat = order[jnp.clip(off[row_e] + local, 0, nk - 1)]
    pad_rank = jnp.cumsum(jnp.logical_not(valid).astype(I32)) - 1
    dst = jnp.where(valid, (flat % TOP_K) * n + flat // TOP_K, nk + pad_rank)
    dst = jnp.concatenate([dst, n_tiles * tm_e + jnp.arange(tm_e, dtype=I32)]).reshape(n_tiles + 1, 1, tm_e)

    y4 = _experts(tile_expert, n_used.astype(I32), dst, h2, p["w1"], p["b1"].reshape(N_EXPERTS, 1, -1),
                  p["w2"], p["b2"].reshape(N_EXPERTS, 1, -1), tm=tm_e)
    return x2, meta, y4


def _tiles(seq, n):
    return dict(tt_a=min(seq, 512), tt_b=min(seq, 512), tm_merge=min(n, 512), tm_router=min(n, 512),
                tm_expert=min(n, 512), tm_combine=min(n, 256))


def kernel(x, norm1_g, w_in, a_conv_w, a_A_log, a_dt_bias, a_norm_g, b_mu, b_w0, b_w2, b_a0, b_a2, b_g2, b_k_k, b_k_a, b_r_k, b_ln_w, b_ln_b, w_up_a, w_up_b, w_out, norm2_g, w_router, b_router, w1, b1, w2, b2, final_g):
    batch, seq, d = x.shape
    n = batch * seq
    depth = norm1_g.shape[0]
    tiles = _tiles(seq, n)
    params = dict(norm1_g=norm1_g, w_in=w_in, a_conv_w=a_conv_w, a_A_log=a_A_log, a_dt_bias=a_dt_bias,
                  a_norm_g=a_norm_g, b_mu=b_mu, b_w0=b_w0, b_w2=b_w2, b_a0=b_a0, b_a2=b_a2, b_g2=b_g2,
                  b_k_k=b_k_k, b_k_a=b_k_a, b_r_k=b_r_k, b_ln_w=b_ln_w, b_ln_b=b_ln_b, w_up_a=w_up_a,
                  w_up_b=w_up_b, w_out=w_out, norm2_g=norm2_g, w_router=w_router, b_router=b_router,
                  w1=w1, b1=b1, w2=w2, b2=b2)
    x2d = x.reshape(n, d)
    for l in range(depth):
        p = {k: v[l] for k, v in params.items()}
        x2, meta, y4 = _layer(x2d, p, batch=batch, seq=seq, tiles=tiles)
        x2d = _combine(x2, meta, final_g.reshape(1, d), y4, tm=tiles["tm_combine"],
                       final_norm=(l == depth - 1))
    return x2d.reshape(batch, seq, d)
```

```python
import functools

import jax
import jax.numpy as jnp
from jax import lax
from jax.experimental import pallas as pl
from jax.experimental.pallas import tpu as pltpu

F32 = jnp.float32
BF16 = jnp.bfloat16
I32 = jnp.int32

A_HEADS = 4
A_HEAD_DIM = 128
A_WIDTH = A_HEADS * A_HEAD_DIM
A_CONV = 4
B_HEADS = 8
B_HEAD_DIM = 64
B_WIDTH = B_HEADS * B_HEAD_DIM
B_DECAY_LORA = 64
B_ICLR_LORA = 64
B_GATE_LORA = 128
B_GN_EPS = 64e-5
N_EXPERTS = 32
TOP_K = 4
SWIGLU_ALPHA = 1.702
SWIGLU_LIMIT = 7.0
RMS_EPS = 1e-5
L2_EPS = 1e-6

LANES = 128
SUBLANES = 8
INV_BLOCK = 16
A_CHUNK = 128
B_CHUNK = 64
VMEM_LIMIT = 56 * 1024 * 1024
DMA_QUEUES = 2


def _dot(a, b):
    return jnp.dot(a.astype(BF16), b.astype(BF16), preferred_element_type=F32)


def _dot_nt(a, b):
    return lax.dot_general(a.astype(BF16), b.astype(BF16), (((1,), (1,)), ((), ())),
                           preferred_element_type=F32)


def _dot_tn(a, b):
    return lax.dot_general(a.astype(BF16), b.astype(BF16), (((0,), (0,)), ((), ())),
                           preferred_element_type=F32)


def _dot_f32(a, b):
    return jnp.dot(a, b, preferred_element_type=F32, precision=lax.Precision.HIGHEST)


def _split_bf16(a, terms):
    out = []
    for _ in range(terms):
        hi = a.astype(BF16)
        out.append(hi)
        a = a - hi.astype(F32)
    return out


def _dot_x3(a, b):
    ah, al = _split_bf16(a, 2)
    bh, bl = _split_bf16(b, 2)
    mm = lambda u, v: jnp.dot(u, v, preferred_element_type=F32)
    return mm(ah, bh) + mm(ah, bl) + mm(al, bh)


def _dot_exact_lhs(a01, b):
    a01 = a01.astype(BF16)
    return sum(jnp.dot(a01, t, preferred_element_type=F32) for t in _split_bf16(b, 3))


def _sigmoid(x):
    return 1.0 / (1.0 + jnp.exp(-x))


def _silu(x):
    return x * _sigmoid(x)


def _softplus(x):
    return jnp.maximum(x, 0.0) + jnp.log(1.0 + jnp.exp(-jnp.abs(x)))


def _rmsnorm(x, g):
    return x * lax.rsqrt(jnp.mean(x * x, axis=-1, keepdims=True) + RMS_EPS) * g


def _iota2(shape, dim):
    return lax.broadcasted_iota(I32, shape, dim)


def _sublane_transpose(planes):
    planes = list(planes)
    sub = lax.broadcasted_iota(I32, planes[0].shape, 1)
    k = SUBLANES // 2
    while k >= 1:
        upper = (sub & k) != 0
        for j in range(SUBLANES):
            if j & k:
                continue
            a, b = planes[j], planes[j | k]
            planes[j] = jnp.where(upper, pltpu.roll(b, k, 1), a)
            planes[j | k] = jnp.where(upper, b, pltpu.roll(a, SUBLANES - k, 1))
        k //= 2
    return planes


def _tiles_to_rows(x3):
    t = x3.shape[0]
    x4 = x3.reshape(t // SUBLANES, SUBLANES, SUBLANES, LANES)
    cols = _sublane_transpose([x4[:, j] for j in range(SUBLANES)])
    return jnp.concatenate([c.reshape(t, LANES) for c in cols], axis=1)


def _rows_to_tiles(x2):
    t = x2.shape[0]
    cols = [x2[:, c * LANES:(c + 1) * LANES].reshape(t // SUBLANES, SUBLANES, LANES) for c in range(SUBLANES)]
    return jnp.stack(_sublane_transpose(cols), axis=1).reshape(t, SUBLANES, LANES)


def _unit_lower_inverses(mats, unit):
    n = mats[0].shape[0]
    row = _iota2((n, n), 0)
    col = _iota2((n, n), 1)
    eye = (row == col).astype(F32)
    same_blk = (row // INV_BLOCK) == (col // INV_BLOCK)
    ps = [jnp.where(same_blk, -a, 0.0) for a in mats]
    dinv = [eye + x for x in ps]
    for _ in range(3):
        ps = [_dot(p, p) for p in ps]
        dinv = [dv + _dot(dv, p) for dv, p in zip(dinv, ps)]
    ps = [-_dot(dv, jnp.where(same_blk, 0.0, a)) for dv, a in zip(dinv, mats)]
    ts = [eye + y for y in ps]
    k = 2
    while k < unit // INV_BLOCK:
        ps = [_dot(p, p) for p in ps]
        ts = [t + _dot(t, p) for t, p in zip(ts, ps)]
        k *= 2
    return [_dot(t, dv) for t, dv in zip(ts, dinv)]


def _gdn_kernel(x_ref, g1_ref, wa_ref, convw_ref, apar_ref, ng_ref, oa_ref,
                ext_ref, s_ref, q_s, k_s, v_s, o_s, beta_s, gc_s, u_s, wq_s, attn_s, kg_s, egl_s, *, tt, chunk):
    qkv_w = 3 * A_WIDTH

    @pl.when(pl.program_id(1) == 0)
    def _():
        ext_ref[0:SUBLANES, :] = jnp.zeros((SUBLANES, qkv_w), F32)
        s_ref[...] = jnp.zeros(s_ref.shape, F32)

    h = _rmsnorm(x_ref[...], g1_ref[...]).astype(BF16)
    pa = jnp.dot(h, wa_ref[...], preferred_element_type=F32)

    ext_ref[SUBLANES:SUBLANES + tt, :] = pa[:, :qkv_w]
    conv = jnp.zeros((tt, qkv_w), F32)
    for j in range(A_CONV):
        off = SUBLANES - (A_CONV - 1) + j
        conv = conv + convw_ref[j:j + 1, :] * ext_ref[off:off + tt, :]
    ext_ref[0:SUBLANES, :] = ext_ref[tt:tt + SUBLANES, :]
    qkv = _silu(conv)

    for hd in range(A_HEADS):
        lo = hd * A_HEAD_DIM
        qh = qkv[:, lo:lo + A_HEAD_DIM]
        kh = qkv[:, A_WIDTH + lo:A_WIDTH + lo + A_HEAD_DIM]
        qn = qh * lax.rsqrt(jnp.sum(qh * qh, axis=-1, keepdims=True) + L2_EPS) * (A_HEAD_DIM ** -0.5)
        kn = kh * lax.rsqrt(jnp.sum(kh * kh, axis=-1, keepdims=True) + L2_EPS)
        q_s[:, lo:lo + A_HEAD_DIM] = qn
        k_s[:, lo:lo + A_HEAD_DIM] = kn
    v_s[...] = qkv[:, 2 * A_WIDTH:]

    bg = pa[:, 4 * A_WIDTH:]
    beta_s[...] = _sigmoid(bg)
    g_all = apar_ref[0:1, :] * _softplus(bg + apar_ref[1:2, :])
    ri = _iota2((chunk, chunk), 0)
    ci = _iota2((chunk, chunk), 1)
    causal = ci <= ri
    strict = ci < ri
    tri = causal.astype(F32)
    for c in range(tt // chunk):
        gc_s[c * chunk:(c + 1) * chunk, :] = _dot_exact_lhs(tri, g_all[c * chunk:(c + 1) * chunk, :])

    heads = range(A_HEADS)

    def prep_step(c, carry):
        r0 = pl.multiple_of(c * chunk, chunk)
        rows = pl.ds(r0, chunk)
        gcc = gc_s[rows, :]
        gct = gcc.T
        bc = beta_s[rows, :]
        cols = [slice(hd * A_HEAD_DIM, (hd + 1) * A_HEAD_DIM) for hd in heads]
        q = [q_s[rows, cs] for cs in cols]
        k = [k_s[rows, cs] for cs in cols]
        v = [v_s[rows, cs] for cs in cols]
        beta = [bc[:, hd:hd + 1] for hd in heads]
        gcol = [gcc[:, A_HEADS + hd:A_HEADS + hd + 1] for hd in heads]
        grow = [gct[A_HEADS + hd:A_HEADS + hd + 1, :] for hd in heads]
        glast = [gcc[chunk - 1:chunk, A_HEADS + hd:A_HEADS + hd + 1] for hd in heads]
        decay = [jnp.where(causal, jnp.exp(jnp.where(causal, gcol[hd] - grow[hd], 0.0)), 0.0) for hd in heads]
        eg = [jnp.exp(gcol[hd]) for hd in heads]
        kb = [k[hd] * beta[hd] for hd in heads]
        sc = [_dot_nt(jnp.concatenate([kb[hd], q[hd]], axis=0), k[hd]) for hd in heads]
        lmat = [jnp.where(strict, sc[hd][:chunk] * decay[hd], 0.0) for hd in heads]
        tmat = _unit_lower_inverses(lmat, chunk)
        uw = [_dot(tmat[hd], jnp.concatenate([v[hd] * beta[hd], kb[hd] * eg[hd]], axis=1)) for hd in heads]
        for hd in heads:
            j = c * A_HEADS + hd
            u_s[j] = uw[hd][:, :A_HEAD_DIM]
            wq_s[j] = jnp.concatenate([uw[hd][:, A_HEAD_DIM:], q[hd] * eg[hd]], axis=0).astype(BF16)
            attn_s[j] = (sc[hd][chunk:] * decay[hd]).astype(BF16)
            kg_s[j] = (k[hd] * jnp.exp(glast[hd] - gcol[hd])).astype(BF16)
            egl_s[j] = jnp.broadcast_to(jnp.exp(glast[hd]), (SUBLANES, LANES))
        return carry

    def scan_step(c, carry):
        r0 = pl.multiple_of(c * chunk, chunk)
        js = [c * A_HEADS + hd for hd in heads]
        s = [s_ref[hd] for hd in heads]
        ws = [_dot(wq_s[js[hd]], s[hd]) for hd in heads]
        v_new = [u_s[js[hd]] - ws[hd][:chunk] for hd in heads]
        o = [ws[hd][chunk:] + _dot(attn_s[js[hd]], v_new[hd]) for hd in heads]
        for hd in heads:
            s_ref[hd] = s[hd] * egl_s[js[hd]][0:1, :] + _dot_tn(kg_s[js[hd]], v_new[hd])
            o_s[pl.ds(r0, chunk), hd * A_HEAD_DIM:(hd + 1) * A_HEAD_DIM] = o[hd]
        return carry

    lax.fori_loop(0, tt // chunk, prep_step, 0)
    lax.fori_loop(0, tt // chunk, scan_step, 0)

    z = pa[:, 3 * A_WIDTH:4 * A_WIDTH]
    for hd in range(A_HEADS):
        lo = hd * A_HEAD_DIM
        o = o_s[:, lo:lo + A_HEAD_DIM]
        o = o * lax.rsqrt(jnp.mean(o * o, axis=-1, keepdims=True) + L2_EPS) * ng_ref[...]
        oa_ref[:, lo:lo + A_HEAD_DIM] = (o * _silu(z[:, lo:lo + A_HEAD_DIM])).astype(oa_ref.dtype)


def _gdn(x2d, norm1_g, w_a, conv_w, a_par, norm_g, *, batch, seq, tt, chunk):
    n, d = x2d.shape
    nt = seq // tt
    cols = w_a.shape[1]
    nj = tt // chunk * A_HEADS
    kern = functools.partial(_gdn_kernel, tt=tt, chunk=chunk)
    const = lambda b, t: (0, 0)
    return pl.pallas_call(
        kern,
        out_shape=jax.ShapeDtypeStruct((n, A_WIDTH), BF16),
        grid=(batch, nt),
        in_specs=[
            pl.BlockSpec((tt, d), lambda b, t: (b * nt + t, 0)),
            pl.BlockSpec((1, d), const),
            pl.BlockSpec((d, cols), const),
            pl.BlockSpec((A_CONV, 3 * A_WIDTH), const),
            pl.BlockSpec((SUBLANES, LANES), const),
            pl.BlockSpec((1, A_HEAD_DIM), const),
        ],
        out_specs=pl.BlockSpec((tt, A_WIDTH), lambda b, t: (b * nt + t, 0)),
        scratch_shapes=[
            pltpu.VMEM((tt + SUBLANES, 3 * A_WIDTH), F32),
            pltpu.VMEM((A_HEADS, A_HEAD_DIM, A_HEAD_DIM), F32),
            pltpu.VMEM((tt, A_WIDTH), F32),
            pltpu.VMEM((tt, A_WIDTH), F32),
            pltpu.VMEM((tt, A_WIDTH), F32),
            pltpu.VMEM((tt, A_WIDTH), F32),
            pltpu.VMEM((tt, LANES), F32),
            pltpu.VMEM((tt, LANES), F32),
            pltpu.VMEM((nj, chunk, A_HEAD_DIM), F32),
            pltpu.VMEM((nj, 2 * chunk, A_HEAD_DIM), BF16),
            pltpu.VMEM((nj, chunk, chunk), BF16),
            pltpu.VMEM((nj, chunk, A_HEAD_DIM), BF16),
            pltpu.VMEM((nj, SUBLANES, LANES), F32),
        ],
        compiler_params=pltpu.CompilerParams(
            dimension_semantics=("parallel", "arbitrary"), vmem_limit_bytes=VMEM_LIMIT),
        name="gdn",
    )(x2d, norm1_g, w_a, conv_w, a_par, norm_g)


def _rwkv_kernel(x_ref, g1_ref, wb_ref, mu_ref, wlora_ref, g2_ref, vec_ref, hsum_ref, ob_ref,
                 ext_ref, s_ref, ar_s, kb_s, hh_s, vs_s, av_s, arb_s, t_s, o_s, wtot_s, *, tt, chunk):
    bcols = 3 * B_WIDTH + B_DECAY_LORA + B_ICLR_LORA + B_GATE_LORA
    npairs = B_WIDTH // LANES

    @pl.when(pl.program_id(1) == 0)
    def _():
        ext_ref[0:SUBLANES, :] = jnp.zeros((SUBLANES, bcols), F32)
        s_ref[...] = jnp.zeros(s_ref.shape, F32)

    h = _rmsnorm(x_ref[...], g1_ref[...]).astype(BF16)
    pb = jnp.dot(h, wb_ref[...], preferred_element_type=F32)

    ext_ref[SUBLANES:SUBLANES + tt, :] = pb
    prev = ext_ref[SUBLANES - 1:SUBLANES - 1 + tt, :]
    ext_ref[0:SUBLANES, :] = ext_ref[tt:tt + SUBLANES, :]
    xm = pb + (prev - pb) * mu_ref[...]

    r = xm[:, 0:B_WIDTH]
    k = xm[:, B_WIDTH:2 * B_WIDTH]
    v = xm[:, 2 * B_WIDTH:3 * B_WIDTH]
    xwa = xm[:, 3 * B_WIDTH:3 * B_WIDTH + LANES]
    xg = xm[:, 3 * B_WIDTH + LANES:]
    lane = _iota2((tt, LANES), 1)
    lora_in = jnp.where(lane < B_DECAY_LORA, jnp.tanh(xwa), xwa)
    lora = _dot_x3(lora_in, wlora_ref[...])
    w0 = vec_ref[0:1, :]
    a0 = vec_ref[1:2, :]
    k_k = vec_ref[2:3, :]
    k_a = vec_ref[3:4, :]
    r_k = vec_ref[4:5, :]
    ln_w = vec_ref[5:6, :]
    ln_b = vec_ref[6:7, :]
    wl = -_softplus(-(w0 + lora[:, :B_WIDTH])) - 0.5
    lw = -jnp.exp(wl)
    a = _sigmoid(a0 + lora[:, B_WIDTH:])
    gate = _dot(_sigmoid(xg), g2_ref[...])

    hsum = hsum_ref[...]

    def head_sum(t):
        return jnp.dot(t.astype(BF16), hsum, preferred_element_type=F32)

    kk = k * k_k
    kk = kk * lax.rsqrt(head_sum(kk * kk) + L2_EPS)
    kf = k * (1.0 + (a - 1.0) * k_a)
    beta = kk * a

    lane2 = _iota2((chunk, LANES), 1)
    m0 = lane2 < B_HEAD_DIM

    def stack(*ts):
        parts = []
        for t in ts:
            parts += [jnp.where(m0, t, 0.0), jnp.where(m0, 0.0, t)]
        return jnp.concatenate(parts, axis=0).astype(BF16)

    tri = (_iota2((chunk, chunk), 1) <= _iota2((chunk, chunk), 0)).astype(F32)
    for c in range(tt // chunk):
        rs = slice(c * chunk, (c + 1) * chunk)
        lwc = lw[rs]
        gcum = _dot_exact_lhs(tri, lwc)
        gtot = gcum[chunk - 1:chunk, :]
        w_inv = jnp.exp(-gcum)
        w_rem = jnp.exp(gtot - gcum)
        at = kk[rs] * jnp.exp(gcum - lwc)
        kt = kf[rs] * w_inv
        bt = beta[rs] * w_inv
        rt = r[rs] * jnp.exp(gcum)
        kh = kf[rs] * w_rem
        bh = beta[rs] * w_rem
        vc = v[rs]
        wtot = jnp.broadcast_to(jnp.exp(gtot), (SUBLANES, B_WIDTH))
        for p in range(npairs):
            ls = slice(p * LANES, (p + 1) * LANES)
            j = c * npairs + p
            ar_s[j] = stack(at[:, ls], rt[:, ls])
            kb_s[j] = stack(kt[:, ls], bt[:, ls])
            hh_s[j] = stack(kh[:, ls], bh[:, ls])
            vs_s[j] = stack(vc[:, ls])
            wtot_s[j] = wtot[:, ls]

    n2 = 2 * chunk
    ri = _iota2((n2, n2), 0)
    ci = _iota2((n2, n2), 1)
    incl = ci <= ri
    strict = ci < ri
    pairs = range(npairs)

    def prep_step(c, carry):
        js = [c * npairs + p for p in pairs]
        sc = [_dot_nt(ar_s[j], kb_s[j]) for j in js]
        a_ab = [jnp.where(strict, s4[:n2, n2:], 0.0) for s4 in sc]
        tmat = _unit_lower_inverses(a_ab, chunk)
        for p in pairs:
            j = js[p]
            s4 = sc[p]
            a_kk = jnp.concatenate([jnp.where(strict, s4[:n2, :n2], 0.0),
                                    jnp.where(incl, s4[n2:, :n2], 0.0)], axis=0)
            av_s[j] = _dot(a_kk, vs_s[j])
            arb_s[j] = jnp.where(incl, s4[n2:, n2:], 0.0).astype(BF16)
            t_s[j] = tmat[p].astype(BF16)
        return carry

    def scan_step(c, carry):
        r0 = pl.multiple_of(c * chunk, chunk)
        js = [c * npairs + p for p in pairs]
        s = [s_ref[p] for p in pairs]
        ps = [_dot_nt(ar_s[js[p]], s[p]) + av_s[js[p]] for p in pairs]
        u = [_dot(t_s[js[p]], ps[p][:n2]) for p in pairs]
        o2 = [ps[p][n2:] - _dot(arb_s[js[p]], u[p]) for p in pairs]
        for p in pairs:
            vu = jnp.concatenate([vs_s[js[p]], (-u[p]).astype(BF16)], axis=0)
            s_ref[p] = s[p] * wtot_s[js[p]][0:1, :] + _dot_tn(vu, hh_s[js[p]])
            o_s[pl.ds(r0, chunk), p * LANES:(p + 1) * LANES] = o2[p][:chunk] + o2[p][chunk:]
        return carry

    lax.fori_loop(0, tt // chunk, prep_step, 0)
    lax.fori_loop(0, tt // chunk, scan_step, 0)

    out = o_s[...]
    mean = head_sum(out) * (1.0 / B_HEAD_DIM)
    cen = out - mean
    var = head_sum(cen * cen) * (1.0 / B_HEAD_DIM)
    out = cen * lax.rsqrt(var + B_GN_EPS) * ln_w + ln_b
    bonus = head_sum(r * kf * r_k) * v
    ob_ref[...] = ((out + bonus) * gate).astype(ob_ref.dtype)


def _rwkv(x2d, norm1_g, w_b, mu, w_lora, g2, vecs, hsum, *, batch, seq, tt, chunk):
    n, d = x2d.shape
    nt = seq // tt
    bcols = w_b.shape[1]
    kern = functools.partial(_rwkv_kernel, tt=tt, chunk=chunk)
    const = lambda b, t: (0, 0)
    nj = tt // chunk * (B_WIDTH // LANES)
    return pl.pallas_call(
        kern,
        out_shape=jax.ShapeDtypeStruct((n, B_WIDTH), BF16),
        grid=(batch, nt),
        in_specs=[
            pl.BlockSpec((tt, d), lambda b, t: (b * nt + t, 0)),
            pl.BlockSpec((1, d), const),
            pl.BlockSpec((d, bcols), const),
            pl.BlockSpec((1, bcols), const),
            pl.BlockSpec((LANES, 2 * B_WIDTH), const),
            pl.BlockSpec((B_GATE_LORA, B_WIDTH), const),
            pl.BlockSpec((SUBLANES, B_WIDTH), const),
            pl.BlockSpec((B_WIDTH, B_WIDTH), const),
        ],
        out_specs=pl.BlockSpec((tt, B_WIDTH), lambda b, t: (b * nt + t, 0)),
        scratch_shapes=[
            pltpu.VMEM((tt + SUBLANES, bcols), F32),
            pltpu.VMEM((B_WIDTH // LANES, LANES, LANES), F32),
            pltpu.VMEM((nj, 4 * chunk, LANES), BF16),
            pltpu.VMEM((nj, 4 * chunk, LANES), BF16),
            pltpu.VMEM((nj, 4 * chunk, LANES), BF16),
            pltpu.VMEM((nj, 2 * chunk, LANES), BF16),
            pltpu.VMEM((nj, 4 * chunk, LANES), F32),
            pltpu.VMEM((nj, 2 * chunk, 2 * chunk), BF16),
            pltpu.VMEM((nj, 2 * chunk, 2 * chunk), BF16),
            pltpu.VMEM((tt, B_WIDTH), F32),
            pltpu.VMEM((nj, SUBLANES, LANES), F32),
        ],
        compiler_params=pltpu.CompilerParams(
            dimension_semantics=("parallel", "arbitrary"), vmem_limit_bytes=VMEM_LIMIT),
        name="rwkv",
    )(x2d, norm1_g, w_b, mu, w_lora, g2, vecs, hsum)


def _merge_kernel(x_ref, g1_ref, oa_ref, ob_ref, wg_ref, wua_ref, wub_ref, wo_ref, x2_ref):
    d = x_ref.shape[1]
    x = x_ref[...]
    h = _rmsnorm(x, g1_ref[...]).astype(BF16)
    pg = jnp.dot(h, wg_ref[...], preferred_element_type=F32)
    ya = jnp.dot(oa_ref[...], wua_ref[...], preferred_element_type=F32)
    yb = jnp.dot(ob_ref[...], wub_ref[...], preferred_element_type=F32)
    merged = _sigmoid(pg[:, :d]) * ya + _sigmoid(pg[:, d:]) * yb
    x2_ref[...] = x + jnp.dot(merged.astype(BF16), wo_ref[...], preferred_element_type=F32)


def _merge(x2d, norm1_g, oa, ob, w_g, w_up_a, w_up_b, w_out, *, tm):
    n, d = x2d.shape
    const = lambda i: (0, 0)
    rows = lambda i: (i, 0)
    return pl.pallas_call(
        _merge_kernel,
        out_shape=jax.ShapeDtypeStruct((n, d), F32),
        grid=(n // tm,),
        in_specs=[
            pl.BlockSpec((tm, d), rows),
            pl.BlockSpec((1, d), const),
            pl.BlockSpec((tm, A_WIDTH), rows),
            pl.BlockSpec((tm, B_WIDTH), rows),
            pl.BlockSpec((d, 2 * d), const),
            pl.BlockSpec((A_WIDTH, d), const),
            pl.BlockSpec((B_WIDTH, d), const),
            pl.BlockSpec((d, d), const),
        ],
        out_specs=pl.BlockSpec((tm, d), rows),
        compiler_params=pltpu.CompilerParams(
            dimension_semantics=("parallel",), vmem_limit_bytes=VMEM_LIMIT),
        name="merge",
    )(x2d, norm1_g, oa, ob, w_g, w_up_a, w_up_b, w_out)


def _router_kernel(x2_ref, g2_ref, wr_ref, br_ref, h2_ref, meta_ref, cnt_ref, run_ref):
    tm = x2_ref.shape[0]

    @pl.when(pl.program_id(0) == 0)
    def _():
        run_ref[...] = jnp.zeros(run_ref.shape, F32)

    h2 = _rmsnorm(x2_ref[...], g2_ref[...])
    h2_ref[...] = _rows_to_tiles(h2)
    logits = _dot_f32(h2, wr_ref[...]) + br_ref[...]
    lane = _iota2((tm, LANES), 1)
    neg = jnp.float32(-jnp.inf)

    work = logits
    sel_idx = []
    sel_val = []
    multi = jnp.zeros((tm, LANES), F32)
    for _ in range(TOP_K):
        mx = jnp.max(work, axis=-1, keepdims=True)
        idx = jnp.min(jnp.where(work == mx, lane, LANES), axis=-1, keepdims=True)
        hit = lane == idx
        multi = multi + hit.astype(F32)
        work = jnp.where(hit, neg, work)
        sel_idx.append(idx)
        sel_val.append(mx)
    ex = [jnp.exp(vv - sel_val[0]) for vv in sel_val]
    den = ex[0] + ex[1] + ex[2] + ex[3]
    gates = [e / den for e in ex]

    run_ref[0:1, :] = run_ref[0:1, :] + jnp.sum(multi, axis=0, keepdims=True)
    cnt_ref[...] = run_ref[...]

    meta = jnp.zeros((tm, LANES), F32)
    for kk in range(TOP_K):
        meta = jnp.where(lane == kk, sel_idx[kk].astype(F32), meta)
        meta = jnp.where(lane == TOP_K + kk, gates[kk], meta)
    meta_ref[...] = meta


def _router(x2, norm2_g, w_r, b_r, *, tm):
    n, d = x2.shape
    const = lambda i: (0, 0)
    rows = lambda i: (i, 0)
    return pl.pallas_call(
        _router_kernel,
        out_shape=(jax.ShapeDtypeStruct((n, SUBLANES, LANES), F32),
                   jax.ShapeDtypeStruct((n, LANES), F32),
                   jax.ShapeDtypeStruct((SUBLANES, LANES), F32)),
        grid=(n // tm,),
        in_specs=[
            pl.BlockSpec((tm, d), rows),
            pl.BlockSpec((1, d), const),
            pl.BlockSpec((d, LANES), const),
            pl.BlockSpec((1, LANES), const),
        ],
        out_specs=(pl.BlockSpec((tm, SUBLANES, LANES), lambda i: (i, 0, 0)),
                   pl.BlockSpec((tm, LANES), rows),
                   pl.BlockSpec((SUBLANES, LANES), const)),
        scratch_shapes=[pltpu.VMEM((SUBLANES, LANES), F32)],
        compiler_params=pltpu.CompilerParams(
            dimension_semantics=("arbitrary",), vmem_limit_bytes=VMEM_LIMIT),
        name="router",
    )(x2, norm2_g, w_r, b_r)


def _experts_kernel(te_ref, nu_ref, cur_ref, nxt_ref, prv_ref, h2_ref, w1_ref, b1_ref, w2_ref, b2_ref, y4_ref,
                    xbuf, ybuf, gsem, ssem):
    del te_ref
    i = pl.program_id(0)
    tm = xbuf.shape[1]
    d = w2_ref.shape[2]
    de = w2_ref.shape[1]
    n_tok = h2_ref.shape[0]
    n_used = nu_ref[0]
    first_spare = y4_ref.shape[0] - tm
    unroll = 8
    nblk = 4

    def token_of(dst_row):
        return dst_row & (n_tok - 1) if n_tok & (n_tok - 1) == 0 else lax.rem(dst_row, n_tok)

    def gather_row(idx_ref, slot, r):
        return pltpu.make_async_copy(h2_ref.at[token_of(idx_ref[0, 0, r])], xbuf.at[slot, r], gsem.at[slot])

    def scatter_row(dst_row, slot, r):
        return pltpu.make_async_copy(ybuf.at[slot, r], y4_ref.at[dst_row], ssem.at[slot])

    def gather_wait_all(slot):
        for r in range(tm):
            pltpu.make_async_copy(h2_ref.at[0], xbuf.at[slot, r], gsem.at[slot]).wait()

    def scatter_wait_all(slot):
        for r in range(tm):
            scatter_row(0, slot, r).wait()

    def rows_loop(fn):
        def body(g, carry):
            for u in range(unroll):
                fn(g * unroll + u, u)
            return carry
        lax.fori_loop(0, tm // unroll, body, 0)

    @pl.when(i == 0)
    def _():
        ybuf[...] = jnp.zeros(ybuf.shape, F32)
        rows_loop(lambda r, u: gather_row(cur_ref, 0, r).start(priority=u % DMA_QUEUES))
        rows_loop(lambda r, u: scatter_row(first_spare + r, 0, r).start(priority=u % DMA_QUEUES))

    @pl.when(i < n_used)
    def _():
        slot = i % 2
        gather_wait_all(slot)
        x = _tiles_to_rows(xbuf[slot]).astype(BF16)
        bw = de // nblk
        acts = []
        for j in range(nblk):
            for r in range(j * tm // nblk, (j + 1) * tm // nblk):
                gather_row(nxt_ref, 1 - slot, r).start(priority=r % DMA_QUEUES)
            cg = slice(j * bw, (j + 1) * bw)
            cl = slice(de + j * bw, de + (j + 1) * bw)
            hg = jnp.dot(x, w1_ref[0, :, cg].astype(BF16), preferred_element_type=F32) + b1_ref[0, :, cg]
            hl = jnp.dot(x, w1_ref[0, :, cl].astype(BF16), preferred_element_type=F32) + b1_ref[0, :, cl]
            glu = jnp.minimum(hg, SWIGLU_LIMIT)
            lin = jnp.clip(hl, -SWIGLU_LIMIT, SWIGLU_LIMIT)
            acts.append((glu * _sigmoid(SWIGLU_ALPHA * glu) * (lin + 1.0)).astype(BF16))
        act = jnp.concatenate(acts, axis=1)
        bw = d // nblk
        ys = []
        for j in range(nblk):
            for r in range(j * tm // nblk, (j + 1) * tm // nblk):
                scatter_row(prv_ref[0, 0, r], 1 - slot, r).start(priority=r % DMA_QUEUES)
            cs = slice(j * bw, (j + 1) * bw)
            ys.append(jnp.dot(act, w2_ref[0, :, cs].astype(BF16), preferred_element_type=F32) + b2_ref[0, :, cs])
        scatter_wait_all(slot)
        ybuf[slot] = _rows_to_tiles(jnp.concatenate(ys, axis=1))

    @pl.when(i == n_used - 1)
    def _():
        slot = i % 2
        scatter_wait_all(1 - slot)
        rows_loop(lambda r, u: scatter_row(cur_ref[0, 0, r], slot, r).start(priority=u % DMA_QUEUES))
        scatter_wait_all(slot)
        gather_wait_all(1 - slot)


def _experts(tile_expert, n_used, dst, h2, w1, b1, w2, b2, *, tm):
    n = h2.shape[0]
    d = w1.shape[1]
    assert h2.shape[1:] == (SUBLANES, LANES) and d == SUBLANES * LANES, "one (8, 128) tile per token row"
    n_tiles = dst.shape[0] - 1
    de2 = w1.shape[2]
    de = w2.shape[1]
    wsel = lambda i, te, nu: (te[i], 0, 0)
    cur = lambda i, te, nu: (jnp.minimum(i, nu[0] - 1), 0, 0)
    nxt = lambda i, te, nu: (jnp.minimum(i + 1, nu[0] - 1), 0, 0)
    prv = lambda i, te, nu: (jnp.where(i == 0, n_tiles, jnp.minimum(i, nu[0]) - 1), 0, 0)
    idx_spec = lambda fn: pl.BlockSpec((1, 1, tm), fn, memory_space=pltpu.SMEM)
    return pl.pallas_call(
        _experts_kernel,
        out_shape=jax.ShapeDtypeStruct(((n_tiles + 2) * tm, SUBLANES, LANES), F32),
        grid_spec=pltpu.PrefetchScalarGridSpec(
            num_scalar_prefetch=2,
            grid=(n_tiles,),
            in_specs=[idx_spec(cur), idx_spec(nxt), idx_spec(prv),
                      pl.BlockSpec(memory_space=pl.ANY),
                      pl.BlockSpec((1, d, de2), wsel),
                      pl.BlockSpec((1, 1, de2), wsel),
                      pl.BlockSpec((1, de, d), wsel),
                      pl.BlockSpec((1, 1, d), wsel)],
            out_specs=pl.BlockSpec(memory_space=pl.ANY),
            scratch_shapes=[pltpu.VMEM((2, tm, SUBLANES, LANES), F32),
                            pltpu.VMEM((2, tm, SUBLANES, LANES), F32),
                            pltpu.SemaphoreType.DMA((2,)),
                            pltpu.SemaphoreType.DMA((2,))],
        ),
        compiler_params=pltpu.CompilerParams(
            dimension_semantics=("arbitrary",), vmem_limit_bytes=VMEM_LIMIT, has_side_effects=True),
        name="experts",
    )(tile_expert, n_used, dst, dst, dst, h2, w1, b1, w2, b2)


def _combine_kernel(x2_ref, meta_ref, fg_ref, *refs, final_norm):
    y_refs, out_ref = refs[:TOP_K], refs[TOP_K]
    meta = meta_ref[...]
    acc = x2_ref[...]
    for kk in range(TOP_K):
        acc = acc + _tiles_to_rows(y_refs[kk][...]) * meta[:, TOP_K + kk:TOP_K + kk + 1]
    out_ref[...] = _rmsnorm(acc, fg_ref[...]) if final_norm else acc


def _combine(x2, meta, final_g, y4, *, tm, final_norm):
    n, d = x2.shape
    nb = n // tm
    rows = lambda i: (i, 0)
    slot_rows = lambda kk: (lambda i: (kk * nb + i, 0, 0))
    return pl.pallas_call(
        functools.partial(_combine_kernel, final_norm=final_norm),
        out_shape=jax.ShapeDtypeStruct((n, d), F32),
        grid=(nb,),
        in_specs=[pl.BlockSpec((tm, d), rows),
                  pl.BlockSpec((tm, LANES), rows),
                  pl.BlockSpec((1, d), lambda i: (0, 0))]
                 + [pl.BlockSpec((tm, SUBLANES, LANES), slot_rows(kk)) for kk in range(TOP_K)],
        out_specs=pl.BlockSpec((tm, d), rows),
        compiler_params=pltpu.CompilerParams(
            dimension_semantics=("parallel",), vmem_limit_bytes=VMEM_LIMIT),
        name="combine",
    )(x2, meta, final_g, *([y4] * TOP_K))


def _pad_cols(w, cols):
    return jnp.pad(w, ((0, 0), (0, cols - w.shape[1])))


def _layer(x2d, p, *, batch, seq, tiles):
    d = x2d.shape[1]
    a_cols = 4 * A_WIDTH + 2 * A_HEADS
    b_cols = 3 * B_WIDTH + B_DECAY_LORA + B_ICLR_LORA + B_GATE_LORA
    w_in = p["w_in"]
    w_a = _pad_cols(w_in[:, :a_cols], 4 * A_WIDTH + LANES).astype(BF16)
    w_b = w_in[:, a_cols:a_cols + b_cols].astype(BF16)
    w_g = w_in[:, a_cols + b_cols:].astype(BF16)
    g1 = p["norm1_g"].reshape(1, d)

    a_par = jnp.zeros((SUBLANES, LANES), F32)
    a_par = a_par.at[0, A_HEADS:2 * A_HEADS].set(-jnp.exp(p["a_A_log"]))
    a_par = a_par.at[1, A_HEADS:2 * A_HEADS].set(p["a_dt_bias"])
    oa = _gdn(x2d, g1, w_a, p["a_conv_w"], a_par, p["a_norm_g"].reshape(1, A_HEAD_DIM),
              batch=batch, seq=seq, tt=tiles["tt_a"], chunk=A_CHUNK)

    w_lora = jnp.zeros((LANES, 2 * B_WIDTH), F32)
    w_lora = w_lora.at[:B_DECAY_LORA, :B_WIDTH].set(p["b_w2"])
    w_lora = w_lora.at[B_DECAY_LORA:, B_WIDTH:].set(p["b_a2"])
    vecs = jnp.stack([p["b_w0"], p["b_a0"], p["b_k_k"], p["b_k_a"], p["b_r_k"].reshape(-1),
                      p["b_ln_w"], p["b_ln_b"], jnp.zeros((B_WIDTH,), F32)])
    hd = jnp.arange(B_WIDTH) // B_HEAD_DIM
    hsum = (hd[:, None] == hd[None, :]).astype(BF16)
    ob = _rwkv(x2d, g1, w_b, p["b_mu"].reshape(1, b_cols), w_lora, p["b_g2"], vecs, hsum,
               batch=batch, seq=seq, tt=tiles["tt_b"], chunk=B_CHUNK)

    x2 = _merge(x2d, g1, oa, ob, w_g, p["w_up_a"].astype(BF16), p["w_up_b"].astype(BF16),
                p["w_out"].astype(BF16), tm=tiles["tm_merge"])

    w_r = _pad_cols(p["w_router"], LANES)
    b_r = jnp.full((1, LANES), -jnp.inf, F32).at[0, :N_EXPERTS].set(p["b_router"])
    h2, meta, cnt = _router(x2, p["norm2_g"].reshape(1, d), w_r, b_r, tm=tiles["tm_router"])

    tm_e = tiles["tm_expert"]
    n = x2d.shape[0]
    nk = n * TOP_K
    counts = cnt[0, :N_EXPERTS].astype(I32)
    tiles_per = (counts + tm_e - 1) // tm_e
    tile_end = jnp.cumsum(tiles_per)
    pad_off = (tile_end - tiles_per) * tm_e
    off = jnp.cumsum(counts) - counts
    n_tiles = nk // tm_e + N_EXPERTS
    n_used = tile_end[-1:]
    tile_ids = jnp.arange(n_tiles, dtype=I32)
    tile_expert = jnp.minimum(jnp.sum(tile_ids[:, None] >= tile_end[None, :], axis=1), N_EXPERTS - 1).astype(I32)
    last_e = tile_expert[jnp.maximum(n_used[0] - 1, 0)]
    tile_expert = jnp.where(tile_ids < n_used[0], tile_expert, last_e)
    order = jnp.argsort(meta[:, :TOP_K].astype(I32).reshape(-1), stable=True).astype(I32)
    rows = jnp.arange(n_tiles * tm_e, dtype=I32)
    row_e = jnp.repeat(tile_expert, tm_e)
    local = rows - pad_off[row_e]
    valid = (local < counts[row_e]) & (rows < n_used[0] * tm_e)
    flat = order[jnp.clip(off[row_e] + local, 0, nk - 1)]
    pad_rank = jnp.cumsum(jnp.logical_not(valid).astype(I32)) - 1
    dst = jnp.where(valid, (flat % TOP_K) * n + flat // TOP_K, nk + pad_rank)
    dst = jnp.concatenate([dst, n_tiles * tm_e + jnp.arange(tm_e, dtype=I32)]).reshape(n_tiles + 1, 1, tm_e)

    y4 = _experts(tile_expert, n_used.astype(I32), dst, h2, p["w1"], p["b1"].reshape(N_EXPERTS, 1, -1),
                  p["w2"], p["b2"].reshape(N_EXPERTS, 1, -1), tm=tm_e)
    return x2, meta, y4


def _tiles(seq, n):
    return dict(tt_a=min(seq, 512), tt_b=min(seq, 512), tm_merge=min(n, 512), tm_router=min(n, 512),
                tm_expert=min(n, 512), tm_combine=min(n, 256))


def kernel(x, norm1_g, w_in, a_conv_w, a_A_log, a_dt_bias, a_norm_g, b_mu, b_w0, b_w2, b_a0, b_a2, b_g2, b_k_k, b_k_a, b_r_k, b_ln_w, b_ln_b, w_up_a, w_up_b, w_out, norm2_g, w_router, b_router, w1, b1, w2, b2, final_g):
    batch, seq, d = x.shape
    n = batch * seq
    depth = norm1_g.shape[0]
    tiles = _tiles(seq, n)
    params = dict(norm1_g=norm1_g, w_in=w_in, a_conv_w=a_conv_w, a_A_log=a_A_log, a_dt_bias=a_dt_bias,
                  a_norm_g=a_norm_g, b_mu=b_mu, b_w0=b_w0, b_w2=b_w2, b_a0=b_a0, b_a2=b_a2, b_g2=b_g2,
                  b_k_k=b_k_k, b_k_a=b_k_a, b_r_k=b_r_k, b_ln_w=b_ln_w, b_ln_b=b_ln_b, w_up_a=w_up_a,
                  w_up_b=w_up_b, w_out=w_out, norm2_g=norm2_g, w_router=w_router, b_router=b_router,
                  w1=w1, b1=b1, w2=w2, b2=b2)
    x2d = x.reshape(n, d)
    for l in range(depth):
        p = {k: v[l] for k, v in params.items()}
        x2, meta, y4 = _layer(x2d, p, batch=batch, seq=seq, tiles=tiles)
        x2d = _combine(x2, meta, final_g.reshape(1, d), y4, tm=tiles["tm_combine"],
                       final_norm=(l == depth - 1))
    return x2d.reshape(batch, seq, d)
```

```python
import functools

import jax
import jax.numpy as jnp
from jax import lax
from jax.experimental import pallas as pl
from jax.experimental.pallas import tpu as pltpu

F32 = jnp.float32
BF16 = jnp.bfloat16
I32 = jnp.int32

A_HEADS = 4
A_HEAD_DIM = 128
A_WIDTH = A_HEADS * A_HEAD_DIM
A_CONV = 4
B_HEADS = 8
B_HEAD_DIM = 64
B_WIDTH = B_HEADS * B_HEAD_DIM
B_DECAY_LORA = 64
B_ICLR_LORA = 64
B_GATE_LORA = 128
B_GN_EPS = 64e-5
N_EXPERTS = 32
TOP_K = 4
SWIGLU_ALPHA = 1.702
SWIGLU_LIMIT = 7.0
RMS_EPS = 1e-5
L2_EPS = 1e-6

LANES = 128
SUBLANES = 8
INV_BLOCK = 16
A_CHUNK = 128
B_CHUNK = 64
VMEM_LIMIT = 56 * 1024 * 1024
DMA_QUEUES = 2
PREP_CHUNKS = 4


def _dot(a, b):
    return jnp.dot(a.astype(BF16), b.astype(BF16), preferred_element_type=F32)


def _dot_nt(a, b):
    return lax.dot_general(a.astype(BF16), b.astype(BF16), (((1,), (1,)), ((), ())),
                           preferred_element_type=F32)


def _dot_tn(a, b):
    return lax.dot_general(a.astype(BF16), b.astype(BF16), (((0,), (0,)), ((), ())),
                           preferred_element_type=F32)


def _dot_f32(a, b):
    return jnp.dot(a, b, preferred_element_type=F32, precision=lax.Precision.HIGHEST)


def _split_bf16(a, terms):
    out = []
    for _ in range(terms):
        hi = a.astype(BF16)
        out.append(hi)
        a = a - hi.astype(F32)
    return out


def _dot_x3(a, b):
    ah, al = _split_bf16(a, 2)
    bh, bl = _split_bf16(b, 2)
    mm = lambda u, v: jnp.dot(u, v, preferred_element_type=F32)
    return mm(ah, bh) + mm(ah, bl) + mm(al, bh)


def _dot_exact_lhs(a01, b):
    a01 = a01.astype(BF16)
    return sum(jnp.dot(a01, t, preferred_element_type=F32) for t in _split_bf16(b, 3))


def _sigmoid(x):
    return 1.0 / (1.0 + jnp.exp(-x))


def _silu(x):
    return x * _sigmoid(x)


def _softplus(x):
    return jnp.maximum(x, 0.0) + jnp.log(1.0 + jnp.exp(-jnp.abs(x)))


def _rmsnorm(x, g):
    return x * lax.rsqrt(jnp.mean(x * x, axis=-1, keepdims=True) + RMS_EPS) * g


def _iota2(shape, dim):
    return lax.broadcasted_iota(I32, shape, dim)


def _sublane_transpose(planes):
    planes = list(planes)
    sub = lax.broadcasted_iota(I32, planes[0].shape, 1)
    k = SUBLANES // 2
    while k >= 1:
        upper = (sub & k) != 0
        for j in range(SUBLANES):
            if j & k:
                continue
            a, b = planes[j], planes[j | k]
            planes[j] = jnp.where(upper, pltpu.roll(b, k, 1), a)
            planes[j | k] = jnp.where(upper, b, pltpu.roll(a, SUBLANES - k, 1))
        k //= 2
    return planes


def _tiles_to_rows(x3):
    t = x3.shape[0]
    x4 = x3.reshape(t // SUBLANES, SUBLANES, SUBLANES, LANES)
    cols = _sublane_transpose([x4[:, j] for j in range(SUBLANES)])
    return jnp.concatenate([c.reshape(t, LANES) for c in cols], axis=1)


def _rows_to_tiles(x2):
    t = x2.shape[0]
    cols = [x2[:, c * LANES:(c + 1) * LANES].reshape(t // SUBLANES, SUBLANES, LANES) for c in range(SUBLANES)]
    return jnp.stack(_sublane_transpose(cols), axis=1).reshape(t, SUBLANES, LANES)


def _unit_lower_inverses(mats, unit):
    n = mats[0].shape[0]
    row = _iota2((n, n), 0)
    col = _iota2((n, n), 1)
    eye = (row == col).astype(F32)
    same_blk = (row // INV_BLOCK) == (col // INV_BLOCK)
    ps = [jnp.where(same_blk, -a, 0.0) for a in mats]
    dinv = [eye + x for x in ps]
    for _ in range(3):
        ps = [_dot(p, p) for p in ps]
        dinv = [dv + _dot(dv, p) for dv, p in zip(dinv, ps)]
    ps = [-_dot(dv, jnp.where(same_blk, 0.0, a)) for dv, a in zip(dinv, mats)]
    ts = [eye + y for y in ps]
    k = 2
    while k < unit // INV_BLOCK:
        ps = [_dot(p, p) for p in ps]
        ts = [t + _dot(t, p) for t, p in zip(ts, ps)]
        k *= 2
    return [_dot(t, dv) for t, dv in zip(ts, dinv)]


def _gdn_kernel(x_ref, g1_ref, wa_ref, convw_ref, apar_ref, ng_ref, oa_ref,
                ext_ref, s_ref, q_s, k_s, v_s, o_s, beta_s, gc_s, u_s, wq_s, attn_s, kg_s, egl_s, *, tt, chunk):
    qkv_w = 3 * A_WIDTH

    @pl.when(pl.program_id(1) == 0)
    def _():
        ext_ref[0:SUBLANES, :] = jnp.zeros((SUBLANES, qkv_w), F32)
        s_ref[...] = jnp.zeros(s_ref.shape, F32)

    h = _rmsnorm(x_ref[...], g1_ref[...]).astype(BF16)
    pa = jnp.dot(h, wa_ref[...], preferred_element_type=F32)

    ext_ref[SUBLANES:SUBLANES + tt, :] = pa[:, :qkv_w]
    conv = jnp.zeros((tt, qkv_w), F32)
    for j in range(A_CONV):
        off = SUBLANES - (A_CONV - 1) + j
        conv = conv + convw_ref[j:j + 1, :] * ext_ref[off:off + tt, :]
    ext_ref[0:SUBLANES, :] = ext_ref[tt:tt + SUBLANES, :]
    qkv = _silu(conv)

    for hd in range(A_HEADS):
        lo = hd * A_HEAD_DIM
        qh = qkv[:, lo:lo + A_HEAD_DIM]
        kh = qkv[:, A_WIDTH + lo:A_WIDTH + lo + A_HEAD_DIM]
        qn = qh * lax.rsqrt(jnp.sum(qh * qh, axis=-1, keepdims=True) + L2_EPS) * (A_HEAD_DIM ** -0.5)
        kn = kh * lax.rsqrt(jnp.sum(kh * kh, axis=-1, keepdims=True) + L2_EPS)
        q_s[:, lo:lo + A_HEAD_DIM] = qn
        k_s[:, lo:lo + A_HEAD_DIM] = kn
    v_s[...] = qkv[:, 2 * A_WIDTH:]

    bg = pa[:, 4 * A_WIDTH:]
    beta_s[...] = _sigmoid(bg)
    g_all = apar_ref[0:1, :] * _softplus(bg + apar_ref[1:2, :])
    ri = _iota2((chunk, chunk), 0)
    ci = _iota2((chunk, chunk), 1)
    causal = ci <= ri
    strict = ci < ri
    tri = causal.astype(F32)
    for c in range(tt // chunk):
        gc_s[c * chunk:(c + 1) * chunk, :] = _dot_exact_lhs(tri, g_all[c * chunk:(c + 1) * chunk, :])

    heads = range(A_HEADS)

    def prep_step(cc, carry):
        jobs, q, k, v, beta, gcol, grow, glast = [], [], [], [], [], [], [], []
        for u in range(PREP_CHUNKS):
            c = cc * PREP_CHUNKS + u
            rows = pl.ds(pl.multiple_of(c * chunk, chunk), chunk)
            gcc = gc_s[rows, :]
            gct = gcc.T
            bc = beta_s[rows, :]
            for hd in heads:
                cs = slice(hd * A_HEAD_DIM, (hd + 1) * A_HEAD_DIM)
                jobs.append(c * A_HEADS + hd)
                q.append(q_s[rows, cs])
                k.append(k_s[rows, cs])
                v.append(v_s[rows, cs])
                beta.append(bc[:, hd:hd + 1])
                gcol.append(gcc[:, A_HEADS + hd:A_HEADS + hd + 1])
                grow.append(gct[A_HEADS + hd:A_HEADS + hd + 1, :])
                glast.append(gcc[chunk - 1:chunk, A_HEADS + hd:A_HEADS + hd + 1])
        ids = range(len(jobs))
        decay = [jnp.where(causal, jnp.exp(jnp.where(causal, gcol[n] - grow[n], 0.0)), 0.0) for n in ids]
        eg = [jnp.exp(gcol[n]) for n in ids]
        kb = [k[n] * beta[n] for n in ids]
        sc = [_dot_nt(jnp.concatenate([kb[n], q[n]], axis=0), k[n]) for n in ids]
        lmat = [jnp.where(strict, sc[n][:chunk] * decay[n], 0.0) for n in ids]
        tmat = _unit_lower_inverses(lmat, chunk)
        uw = [_dot(tmat[n], jnp.concatenate([v[n] * beta[n], kb[n] * eg[n]], axis=1)) for n in ids]
        for n in ids:
            j = jobs[n]
            u_s[j] = uw[n][:, :A_HEAD_DIM]
            wq_s[j] = jnp.concatenate([uw[n][:, A_HEAD_DIM:], q[n] * eg[n]], axis=0).astype(BF16)
            attn_s[j] = (sc[n][chunk:] * decay[n]).astype(BF16)
            kg_s[j] = (k[n] * jnp.exp(glast[n] - gcol[n])).astype(BF16)
            egl_s[j] = jnp.broadcast_to(jnp.exp(glast[n]), (SUBLANES, LANES))
        return carry

    def scan_step(c, carry):
        r0 = pl.multiple_of(c * chunk, chunk)
        js = [c * A_HEADS + hd for hd in heads]
        s = [s_ref[hd] for hd in heads]
        ws = [_dot(wq_s[js[hd]], s[hd]) for hd in heads]
        v_new = [u_s[js[hd]] - ws[hd][:chunk] for hd in heads]
        o = [ws[hd][chunk:] + _dot(attn_s[js[hd]], v_new[hd]) for hd in heads]
        for hd in heads:
            s_ref[hd] = s[hd] * egl_s[js[hd]][0:1, :] + _dot_tn(kg_s[js[hd]], v_new[hd])
            o_s[pl.ds(r0, chunk), hd * A_HEAD_DIM:(hd + 1) * A_HEAD_DIM] = o[hd]
        return carry

    lax.fori_loop(0, tt // chunk // PREP_CHUNKS, prep_step, 0)
    lax.fori_loop(0, tt // chunk, scan_step, 0)

    z = pa[:, 3 * A_WIDTH:4 * A_WIDTH]
    for hd in range(A_HEADS):
        lo = hd * A_HEAD_DIM
        o = o_s[:, lo:lo + A_HEAD_DIM]
        o = o * lax.rsqrt(jnp.mean(o * o, axis=-1, keepdims=True) + L2_EPS) * ng_ref[...]
        oa_ref[:, lo:lo + A_HEAD_DIM] = (o * _silu(z[:, lo:lo + A_HEAD_DIM])).astype(oa_ref.dtype)


def _gdn(x2d, norm1_g, w_a, conv_w, a_par, norm_g, *, batch, seq, tt, chunk):
    n, d = x2d.shape
    nt = seq // tt
    cols = w_a.shape[1]
    nj = tt // chunk * A_HEADS
    kern = functools.partial(_gdn_kernel, tt=tt, chunk=chunk)
    const = lambda b, t: (0, 0)
    return pl.pallas_call(
        kern,
        out_shape=jax.ShapeDtypeStruct((n, A_WIDTH), BF16),
        grid=(batch, nt),
        in_specs=[
            pl.BlockSpec((tt, d), lambda b, t: (b * nt + t, 0)),
            pl.BlockSpec((1, d), const),
            pl.BlockSpec((d, cols), const),
            pl.BlockSpec((A_CONV, 3 * A_WIDTH), const),
            pl.BlockSpec((SUBLANES, LANES), const),
            pl.BlockSpec((1, A_HEAD_DIM), const),
        ],
        out_specs=pl.BlockSpec((tt, A_WIDTH), lambda b, t: (b * nt + t, 0)),
        scratch_shapes=[
            pltpu.VMEM((tt + SUBLANES, 3 * A_WIDTH), F32),
            pltpu.VMEM((A_HEADS, A_HEAD_DIM, A_HEAD_DIM), F32),
            pltpu.VMEM((tt, A_WIDTH), F32),
            pltpu.VMEM((tt, A_WIDTH), F32),
            pltpu.VMEM((tt, A_WIDTH), F32),
            pltpu.VMEM((tt, A_WIDTH), F32),
            pltpu.VMEM((tt, LANES), F32),
            pltpu.VMEM((tt, LANES), F32),
            pltpu.VMEM((nj, chunk, A_HEAD_DIM), F32),
            pltpu.VMEM((nj, 2 * chunk, A_HEAD_DIM), BF16),
            pltpu.VMEM((nj, chunk, chunk), BF16),
            pltpu.VMEM((nj, chunk, A_HEAD_DIM), BF16),
            pltpu.VMEM((nj, SUBLANES, LANES), F32),
        ],
        compiler_params=pltpu.CompilerParams(
            dimension_semantics=("parallel", "arbitrary"), vmem_limit_bytes=VMEM_LIMIT),
        name="gdn",
    )(x2d, norm1_g, w_a, conv_w, a_par, norm_g)


def _rwkv_kernel(x_ref, g1_ref, wb_ref, mu_ref, wlora_ref, g2_ref, vec_ref, hsum_ref, ob_ref,
                 ext_ref, s_ref, ar_s, kb_s, hh_s, vs_s, av_s, arb_s, t_s, o_s, wtot_s, *, tt, chunk):
    bcols = 3 * B_WIDTH + B_DECAY_LORA + B_ICLR_LORA + B_GATE_LORA
    npairs = B_WIDTH // LANES

    @pl.when(pl.program_id(1) == 0)
    def _():
        ext_ref[0:SUBLANES, :] = jnp.zeros((SUBLANES, bcols), F32)
        s_ref[...] = jnp.zeros(s_ref.shape, F32)

    h = _rmsnorm(x_ref[...], g1_ref[...]).astype(BF16)
    pb = jnp.dot(h, wb_ref[...], preferred_element_type=F32)

    ext_ref[SUBLANES:SUBLANES + tt, :] = pb
    prev = ext_ref[SUBLANES - 1:SUBLANES - 1 + tt, :]
    ext_ref[0:SUBLANES, :] = ext_ref[tt:tt + SUBLANES, :]
    xm = pb + (prev - pb) * mu_ref[...]

    r = xm[:, 0:B_WIDTH]
    k = xm[:, B_WIDTH:2 * B_WIDTH]
    v = xm[:, 2 * B_WIDTH:3 * B_WIDTH]
    xwa = xm[:, 3 * B_WIDTH:3 * B_WIDTH + LANES]
    xg = xm[:, 3 * B_WIDTH + LANES:]
    lane = _iota2((tt, LANES), 1)
    lora_in = jnp.where(lane < B_DECAY_LORA, jnp.tanh(xwa), xwa)
    lora = _dot_x3(lora_in, wlora_ref[...])
    w0 = vec_ref[0:1, :]
    a0 = vec_ref[1:2, :]
    k_k = vec_ref[2:3, :]
    k_a = vec_ref[3:4, :]
    r_k = vec_ref[4:5, :]
    ln_w = vec_ref[5:6, :]
    ln_b = vec_ref[6:7, :]
    wl = -_softplus(-(w0 + lora[:, :B_WIDTH])) - 0.5
    lw = -jnp.exp(wl)
    a = _sigmoid(a0 + lora[:, B_WIDTH:])
    gate = _dot(_sigmoid(xg), g2_ref[...])

    hsum = hsum_ref[...]

    def head_sum(t):
        return jnp.dot(t.astype(BF16), hsum, preferred_element_type=F32)

    kk = k * k_k
    kk = kk * lax.rsqrt(head_sum(kk * kk) + L2_EPS)
    kf = k * (1.0 + (a - 1.0) * k_a)
    beta = kk * a

    lane2 = _iota2((chunk, LANES), 1)
    m0 = lane2 < B_HEAD_DIM

    def stack(*ts):
        parts = []
        for t in ts:
            parts += [jnp.where(m0, t, 0.0), jnp.where(m0, 0.0, t)]
        return jnp.concatenate(parts, axis=0).astype(BF16)

    tri = (_iota2((chunk, chunk), 1) <= _iota2((chunk, chunk), 0)).astype(F32)
    for c in range(tt // chunk):
        rs = slice(c * chunk, (c + 1) * chunk)
        lwc = lw[rs]
        gcum = _dot_exact_lhs(tri, lwc)
        gtot = gcum[chunk - 1:chunk, :]
        w_inv = jnp.exp(-gcum)
        w_rem = jnp.exp(gtot - gcum)
        at = kk[rs] * jnp.exp(gcum - lwc)
        kt = kf[rs] * w_inv
        bt = beta[rs] * w_inv
        rt = r[rs] * jnp.exp(gcum)
        kh = kf[rs] * w_rem
        bh = beta[rs] * w_rem
        vc = v[rs]
        wtot = jnp.broadcast_to(jnp.exp(gtot), (SUBLANES, B_WIDTH))
        for p in range(npairs):
            ls = slice(p * LANES, (p + 1) * LANES)
            j = c * npairs + p
            ar_s[j] = stack(at[:, ls], rt[:, ls])
            kb_s[j] = stack(kt[:, ls], bt[:, ls])
            hh_s[j] = stack(kh[:, ls], bh[:, ls])
            vs_s[j] = stack(vc[:, ls])
            wtot_s[j] = wtot[:, ls]

    n2 = 2 * chunk
    ri = _iota2((n2, n2), 0)
    ci = _iota2((n2, n2), 1)
    incl = ci <= ri
    strict = ci < ri
    pairs = range(npairs)

    def prep_step(cc, carry):
        js = [(cc * PREP_CHUNKS + u) * npairs + p for u in range(PREP_CHUNKS) for p in pairs]
        sc = [_dot_nt(ar_s[j], kb_s[j]) for j in js]
        a_ab = [jnp.where(strict, s4[:n2, n2:], 0.0) for s4 in sc]
        tmat = _unit_lower_inverses(a_ab, chunk)
        for j, s4, tm_j in zip(js, sc, tmat):
            a_kk = jnp.concatenate([jnp.where(strict, s4[:n2, :n2], 0.0),
                                    jnp.where(incl, s4[n2:, :n2], 0.0)], axis=0)
            av_s[j] = _dot(a_kk, vs_s[j])
            arb_s[j] = jnp.where(incl, s4[n2:, n2:], 0.0).astype(BF16)
            t_s[j] = tm_j.astype(BF16)
        return carry

    def scan_step(c, carry):
        r0 = pl.multiple_of(c * chunk, chunk)
        js = [c * npairs + p for p in pairs]
        s = [s_ref[p] for p in pairs]
        ps = [_dot_nt(ar_s[js[p]], s[p]) + av_s[js[p]] for p in pairs]
        u = [_dot(t_s[js[p]], ps[p][:n2]) for p in pairs]
        o2 = [ps[p][n2:] - _dot(arb_s[js[p]], u[p]) for p in pairs]
        for p in pairs:
            vu = jnp.concatenate([vs_s[js[p]], (-u[p]).astype(BF16)], axis=0)
            s_ref[p] = s[p] * wtot_s[js[p]][0:1, :] + _dot_tn(vu, hh_s[js[p]])
            o_s[pl.ds(r0, chunk), p * LANES:(p + 1) * LANES] = o2[p][:chunk] + o2[p][chunk:]
        return carry

    lax.fori_loop(0, tt // chunk // PREP_CHUNKS, prep_step, 0)
    lax.fori_loop(0, tt // chunk, scan_step, 0)

    out = o_s[...]
    mean = head_sum(out) * (1.0 / B_HEAD_DIM)
    cen = out - mean
    var = head_sum(cen * cen) * (1.0 / B_HEAD_DIM)
    out = cen * lax.rsqrt(var + B_GN_EPS) * ln_w + ln_b
    bonus = head_sum(r * kf * r_k) * v
    ob_ref[...] = ((out + bonus) * gate).astype(ob_ref.dtype)


def _rwkv(x2d, norm1_g, w_b, mu, w_lora, g2, vecs, hsum, *, batch, seq, tt, chunk):
    n, d = x2d.shape
    nt = seq // tt
    bcols = w_b.shape[1]
    kern = functools.partial(_rwkv_kernel, tt=tt, chunk=chunk)
    const = lambda b, t: (0, 0)
    nj = tt // chunk * (B_WIDTH // LANES)
    return pl.pallas_call(
        kern,
        out_shape=jax.ShapeDtypeStruct((n, B_WIDTH), BF16),
        grid=(batch, nt),
        in_specs=[
            pl.BlockSpec((tt, d), lambda b, t: (b * nt + t, 0)),
            pl.BlockSpec((1, d), const),
            pl.BlockSpec((d, bcols), const),
            pl.BlockSpec((1, bcols), const),
            pl.BlockSpec((LANES, 2 * B_WIDTH), const),
            pl.BlockSpec((B_GATE_LORA, B_WIDTH), const),
            pl.BlockSpec((SUBLANES, B_WIDTH), const),
            pl.BlockSpec((B_WIDTH, B_WIDTH), const),
        ],
        out_specs=pl.BlockSpec((tt, B_WIDTH), lambda b, t: (b * nt + t, 0)),
        scratch_shapes=[
            pltpu.VMEM((tt + SUBLANES, bcols), F32),
            pltpu.VMEM((B_WIDTH // LANES, LANES, LANES), F32),
            pltpu.VMEM((nj, 4 * chunk, LANES), BF16),
            pltpu.VMEM((nj, 4 * chunk, LANES), BF16),
            pltpu.VMEM((nj, 4 * chunk, LANES), BF16),
            pltpu.VMEM((nj, 2 * chunk, LANES), BF16),
            pltpu.VMEM((nj, 4 * chunk, LANES), F32),
            pltpu.VMEM((nj, 2 * chunk, 2 * chunk), BF16),
            pltpu.VMEM((nj, 2 * chunk, 2 * chunk), BF16),
            pltpu.VMEM((tt, B_WIDTH), F32),
            pltpu.VMEM((nj, SUBLANES, LANES), F32),
        ],
        compiler_params=pltpu.CompilerParams(
            dimension_semantics=("parallel", "arbitrary"), vmem_limit_bytes=VMEM_LIMIT),
        name="rwkv",
    )(x2d, norm1_g, w_b, mu, w_lora, g2, vecs, hsum)


def _merge_kernel(x_ref, g1_ref, oa_ref, ob_ref, wg_ref, wua_ref, wub_ref, wo_ref, x2_ref):
    d = x_ref.shape[1]
    x = x_ref[...]
    h = _rmsnorm(x, g1_ref[...]).astype(BF16)
    pg = jnp.dot(h, wg_ref[...], preferred_element_type=F32)
    ya = jnp.dot(oa_ref[...], wua_ref[...], preferred_element_type=F32)
    yb = jnp.dot(ob_ref[...], wub_ref[...], preferred_element_type=F32)
    merged = _sigmoid(pg[:, :d]) * ya + _sigmoid(pg[:, d:]) * yb
    x2_ref[...] = x + jnp.dot(merged.astype(BF16), wo_ref[...], preferred_element_type=F32)


def _merge(x2d, norm1_g, oa, ob, w_g, w_up_a, w_up_b, w_out, *, tm):
    n, d = x2d.shape
    const = lambda i: (0, 0)
    rows = lambda i: (i, 0)
    return pl.pallas_call(
        _merge_kernel,
        out_shape=jax.ShapeDtypeStruct((n, d), F32),
        grid=(n // tm,),
        in_specs=[
            pl.BlockSpec((tm, d), rows),
            pl.BlockSpec((1, d), const),
            pl.BlockSpec((tm, A_WIDTH), rows),
            pl.BlockSpec((tm, B_WIDTH), rows),
            pl.BlockSpec((d, 2 * d), const),
            pl.BlockSpec((A_WIDTH, d), const),
            pl.BlockSpec((B_WIDTH, d), const),
            pl.BlockSpec((d, d), const),
        ],
        out_specs=pl.BlockSpec((tm, d), rows),
        compiler_params=pltpu.CompilerParams(
            dimension_semantics=("parallel",), vmem_limit_bytes=VMEM_LIMIT),
        name="merge",
    )(x2d, norm1_g, oa, ob, w_g, w_up_a, w_up_b, w_out)


def _router_kernel(x2_ref, g2_ref, wr_ref, br_ref, h2_ref, meta_ref, cnt_ref, run_ref):
    tm = x2_ref.shape[0]

    @pl.when(pl.program_id(0) == 0)
    def _():
        run_ref[...] = jnp.zeros(run_ref.shape, F32)

    h2 = _rmsnorm(x2_ref[...], g2_ref[...])
    h2_ref[...] = _rows_to_tiles(h2)
    logits = _dot_f32(h2, wr_ref[...]) + br_ref[...]
    lane = _iota2((tm, LANES), 1)
    neg = jnp.float32(-jnp.inf)

    work = logits
    sel_idx = []
    sel_val = []
    multi = jnp.zeros((tm, LANES), F32)
    for _ in range(TOP_K):
        mx = jnp.max(work, axis=-1, keepdims=True)
        idx = jnp.min(jnp.where(work == mx, lane, LANES), axis=-1, keepdims=True)
        hit = lane == idx
        multi = multi + hit.astype(F32)
        work = jnp.where(hit, neg, work)
        sel_idx.append(idx)
        sel_val.append(mx)
    ex = [jnp.exp(vv - sel_val[0]) for vv in sel_val]
    den = ex[0] + ex[1] + ex[2] + ex[3]
    gates = [e / den for e in ex]

    run_ref[0:1, :] = run_ref[0:1, :] + jnp.sum(multi, axis=0, keepdims=True)
    cnt_ref[...] = run_ref[...]

    meta = jnp.zeros((tm, LANES), F32)
    for kk in range(TOP_K):
        meta = jnp.where(lane == kk, sel_idx[kk].astype(F32), meta)
        meta = jnp.where(lane == TOP_K + kk, gates[kk], meta)
    meta_ref[...] = meta


def _router(x2, norm2_g, w_r, b_r, *, tm):
    n, d = x2.shape
    const = lambda i: (0, 0)
    rows = lambda i: (i, 0)
    return pl.pallas_call(
        _router_kernel,
        out_shape=(jax.ShapeDtypeStruct((n, SUBLANES, LANES), F32),
                   jax.ShapeDtypeStruct((n, LANES), F32),
                   jax.ShapeDtypeStruct((SUBLANES, LANES), F32)),
        grid=(n // tm,),
        in_specs=[
            pl.BlockSpec((tm, d), rows),
            pl.BlockSpec((1, d), const),
            pl.BlockSpec((d, LANES), const),
            pl.BlockSpec((1, LANES), const),
        ],
        out_specs=(pl.BlockSpec((tm, SUBLANES, LANES), lambda i: (i, 0, 0)),
                   pl.BlockSpec((tm, LANES), rows),
                   pl.BlockSpec((SUBLANES, LANES), const)),
        scratch_shapes=[pltpu.VMEM((SUBLANES, LANES), F32)],
        compiler_params=pltpu.CompilerParams(
            dimension_semantics=("arbitrary",), vmem_limit_bytes=VMEM_LIMIT),
        name="router",
    )(x2, norm2_g, w_r, b_r)


def _experts_kernel(te_ref, nu_ref, cur_ref, nxt_ref, prv_ref, h2_ref, w1_ref, b1_ref, w2_ref, b2_ref, y4_ref,
                    xbuf, ybuf, gsem, ssem):
    del te_ref
    i = pl.program_id(0)
    tm = xbuf.shape[1]
    d = w2_ref.shape[2]
    de = w2_ref.shape[1]
    n_tok = h2_ref.shape[0]
    n_used = nu_ref[0]
    first_spare = y4_ref.shape[0] - tm
    unroll = 8
    nblk = 4

    def token_of(dst_row):
        return dst_row & (n_tok - 1) if n_tok & (n_tok - 1) == 0 else lax.rem(dst_row, n_tok)

    def gather_row(idx_ref, slot, r):
        return pltpu.make_async_copy(h2_ref.at[token_of(idx_ref[0, 0, r])], xbuf.at[slot, r], gsem.at[slot])

    def scatter_row(dst_row, slot, r):
        return pltpu.make_async_copy(ybuf.at[slot, r], y4_ref.at[dst_row], ssem.at[slot])

    def gather_wait_all(slot):
        for r in range(tm):
            pltpu.make_async_copy(h2_ref.at[0], xbuf.at[slot, r], gsem.at[slot]).wait()

    def scatter_wait_all(slot):
        for r in range(tm):
            scatter_row(0, slot, r).wait()

    def rows_loop(fn):
        def body(g, carry):
            for u in range(unroll):
                fn(g * unroll + u, u)
            return carry
        lax.fori_loop(0, tm // unroll, body, 0)

    @pl.when(i == 0)
    def _():
        ybuf[...] = jnp.zeros(ybuf.shape, F32)
        rows_loop(lambda r, u: gather_row(cur_ref, 0, r).start(priority=u % DMA_QUEUES))
        rows_loop(lambda r, u: scatter_row(first_spare + r, 0, r).start(priority=u % DMA_QUEUES))

    @pl.when(i < n_used)
    def _():
        slot = i % 2
        gather_wait_all(slot)
        x = _tiles_to_rows(xbuf[slot]).astype(BF16)
        bw = de // nblk
        acts = []
        for j in range(nblk):
            for r in range(j * tm // nblk, (j + 1) * tm // nblk):
                gather_row(nxt_ref, 1 - slot, r).start(priority=r % DMA_QUEUES)
            cg = slice(j * bw, (j + 1) * bw)
            cl = slice(de + j * bw, de + (j + 1) * bw)
            hg = jnp.dot(x, w1_ref[0, :, cg].astype(BF16), preferred_element_type=F32) + b1_ref[0, :, cg]
            hl = jnp.dot(x, w1_ref[0, :, cl].astype(BF16), preferred_element_type=F32) + b1_ref[0, :, cl]
            glu = jnp.minimum(hg, SWIGLU_LIMIT)
            lin = jnp.clip(hl, -SWIGLU_LIMIT, SWIGLU_LIMIT)
            acts.append((glu * _sigmoid(SWIGLU_ALPHA * glu) * (lin + 1.0)).astype(BF16))
        act = jnp.concatenate(acts, axis=1)
        bw = d // nblk
        ys = []
        for j in range(nblk):
            for r in range(j * tm // nblk, (j + 1) * tm // nblk):
                scatter_row(prv_ref[0, 0, r], 1 - slot, r).start(priority=r % DMA_QUEUES)
            cs = slice(j * bw, (j + 1) * bw)
            ys.append(jnp.dot(act, w2_ref[0, :, cs].astype(BF16), preferred_element_type=F32) + b2_ref[0, :, cs])
        scatter_wait_all(slot)
        ybuf[slot] = _rows_to_tiles(jnp.concatenate(ys, axis=1))

    @pl.when(i == n_used - 1)
    def _():
        slot = i % 2
        scatter_wait_all(1 - slot)
        rows_loop(lambda r, u: scatter_row(cur_ref[0, 0, r], slot, r).start(priority=u % DMA_QUEUES))
        scatter_wait_all(slot)
        gather_wait_all(1 - slot)


def _experts(tile_expert, n_used, dst, h2, w1, b1, w2, b2, *, tm):
    n = h2.shape[0]
    d = w1.shape[1]
    assert h2.shape[1:] == (SUBLANES, LANES) and d == SUBLANES * LANES, "one (8, 128) tile per token row"
    n_tiles = dst.shape[0] - 1
    de2 = w1.shape[2]
    de = w2.shape[1]
    wsel = lambda i, te, nu: (te[i], 0, 0)
    cur = lambda i, te, nu: (jnp.minimum(i, nu[0] - 1), 0, 0)
    nxt = lambda i, te, nu: (jnp.minimum(i + 1, nu[0] - 1), 0, 0)
    prv = lambda i, te, nu: (jnp.where(i == 0, n_tiles, jnp.minimum(i, nu[0]) - 1), 0, 0)
    idx_spec = lambda fn: pl.BlockSpec((1, 1, tm), fn, memory_space=pltpu.SMEM)
    return pl.pallas_call(
        _experts_kernel,
        out_shape=jax.ShapeDtypeStruct(((n_tiles + 2) * tm, SUBLANES, LANES), F32),
        grid_spec=pltpu.PrefetchScalarGridSpec(
            num_scalar_prefetch=2,
            grid=(n_tiles,),
            in_specs=[idx_spec(cur), idx_spec(nxt), idx_spec(prv),
                      pl.BlockSpec(memory_space=pl.ANY),
                      pl.BlockSpec((1, d, de2), wsel),
                      pl.BlockSpec((1, 1, de2), wsel),
                      pl.BlockSpec((1, de, d), wsel),
                      pl.BlockSpec((1, 1, d), wsel)],
            out_specs=pl.BlockSpec(memory_space=pl.ANY),
            scratch_shapes=[pltpu.VMEM((2, tm, SUBLANES, LANES), F32),
                            pltpu.VMEM((2, tm, SUBLANES, LANES), F32),
                            pltpu.SemaphoreType.DMA((2,)),
                            pltpu.SemaphoreType.DMA((2,))],
        ),
        compiler_params=pltpu.CompilerParams(
            dimension_semantics=("arbitrary",), vmem_limit_bytes=VMEM_LIMIT, has_side_effects=True),
        name="experts",
    )(tile_expert, n_used, dst, dst, dst, h2, w1, b1, w2, b2)


def _combine_kernel(x2_ref, meta_ref, fg_ref, *refs, final_norm):
    y_refs, out_ref = refs[:TOP_K], refs[TOP_K]
    meta = meta_ref[...]
    acc = x2_ref[...]
    for kk in range(TOP_K):
        acc = acc + _tiles_to_rows(y_refs[kk][...]) * meta[:, TOP_K + kk:TOP_K + kk + 1]
    out_ref[...] = _rmsnorm(acc, fg_ref[...]) if final_norm else acc


def _combine(x2, meta, final_g, y4, *, tm, final_norm):
    n, d = x2.shape
    nb = n // tm
    rows = lambda i: (i, 0)
    slot_rows = lambda kk: (lambda i: (kk * nb + i, 0, 0))
    return pl.pallas_call(
        functools.partial(_combine_kernel, final_norm=final_norm),
        out_shape=jax.ShapeDtypeStruct((n, d), F32),
        grid=(nb,),
        in_specs=[pl.BlockSpec((tm, d), rows),
                  pl.BlockSpec((tm, LANES), rows),
                  pl.BlockSpec((1, d), lambda i: (0, 0))]
                 + [pl.BlockSpec((tm, SUBLANES, LANES), slot_rows(kk)) for kk in range(TOP_K)],
        out_specs=pl.BlockSpec((tm, d), rows),
        compiler_params=pltpu.CompilerParams(
            dimension_semantics=("parallel",), vmem_limit_bytes=VMEM_LIMIT),
        name="combine",
    )(x2, meta, final_g, *([y4] * TOP_K))


def _pad_cols(w, cols):
    return jnp.pad(w, ((0, 0), (0, cols - w.shape[1])))


def _layer(x2d, p, *, batch, seq, tiles):
    d = x2d.shape[1]
    a_cols = 4 * A_WIDTH + 2 * A_HEADS
    b_cols = 3 * B_WIDTH + B_DECAY_LORA + B_ICLR_LORA + B_GATE_LORA
    w_in = p["w_in"]
    w_a = _pad_cols(w_in[:, :a_cols], 4 * A_WIDTH + LANES).astype(BF16)
    w_b = w_in[:, a_cols:a_cols + b_cols].astype(BF16)
    w_g = w_in[:, a_cols + b_cols:].astype(BF16)
    g1 = p["norm1_g"].reshape(1, d)

    a_par = jnp.zeros((SUBLANES, LANES), F32)
    a_par = a_par.at[0, A_HEADS:2 * A_HEADS].set(-jnp.exp(p["a_A_log"]))
    a_par = a_par.at[1, A_HEADS:2 * A_HEADS].set(p["a_dt_bias"])
    oa = _gdn(x2d, g1, w_a, p["a_conv_w"], a_par, p["a_norm_g"].reshape(1, A_HEAD_DIM),
              batch=batch, seq=seq, tt=tiles["tt_a"], chunk=A_CHUNK)

    w_lora = jnp.zeros((LANES, 2 * B_WIDTH), F32)
    w_lora = w_lora.at[:B_DECAY_LORA, :B_WIDTH].set(p["b_w2"])
    w_lora = w_lora.at[B_DECAY_LORA:, B_WIDTH:].set(p["b_a2"])
    vecs = jnp.stack([p["b_w0"], p["b_a0"], p["b_k_k"], p["b_k_a"], p["b_r_k"].reshape(-1),
                      p["b_ln_w"], p["b_ln_b"], jnp.zeros((B_WIDTH,), F32)])
    hd = jnp.arange(B_WIDTH) // B_HEAD_DIM
    hsum = (hd[:, None] == hd[None, :]).astype(BF16)
    ob = _rwkv(x2d, g1, w_b, p["b_mu"].reshape(1, b_cols), w_lora, p["b_g2"], vecs, hsum,
               batch=batch, seq=seq, tt=tiles["tt_b"], chunk=B_CHUNK)

    x2 = _merge(x2d, g1, oa, ob, w_g, p["w_up_a"].astype(BF16), p["w_up_b"].astype(BF16),
                p["w_out"].astype(BF16), tm=tiles["tm_merge"])

    w_r = _pad_cols(p["w_router"], LANES)
    b_r = jnp.full((1, LANES), -jnp.inf, F32).at[0, :N_EXPERTS].set(p["b_router"])
    h2, meta, cnt = _router(x2, p["norm2_g"].reshape(1, d), w_r, b_r, tm=tiles["tm_router"])

    tm_e = tiles["tm_expert"]
    n = x2d.shape[0]
    nk = n * TOP_K
    counts = cnt[0, :N_EXPERTS].astype(I32)
    tiles_per = (counts + tm_e - 1) // tm_e
    tile_end = jnp.cumsum(tiles_per)
    pad_off = (tile_end - tiles_per) * tm_e
    off = jnp.cumsum(counts) - counts
    n_tiles = nk // tm_e + N_EXPERTS
    n_used = tile_end[-1:]
    tile_ids = jnp.arange(n_tiles, dtype=I32)
    tile_expert = jnp.minimum(jnp.sum(tile_ids[:, None] >= tile_end[None, :], axis=1), N_EXPERTS - 1).astype(I32)
    last_e = tile_expert[jnp.maximum(n_used[0] - 1, 0)]
    tile_expert = jnp.where(tile_ids < n_used[0], tile_expert, last_e)
    order = jnp.argsort(meta[:, :TOP_K].astype(I32).reshape(-1), stable=True).astype(I32)
    rows = jnp.arange(n_tiles * tm_e, dtype=I32)
    row_e = jnp.repeat(tile_expert, tm_e)
    local = rows - pad_off[row_e]
    valid = (local < counts[row_e]) & (rows < n_used[0] * tm_e)
    flat = order[jnp.clip(off[row_e] + local, 0, nk - 1)]
    pad_rank = jnp.cumsum(jnp.logical_not(valid).astype(I32)) - 1
    dst = jnp.where(valid, (flat % TOP_K) * n + flat // TOP_K, nk + pad_rank)
    dst = jnp.concatenate([dst, n_tiles * tm_e + jnp.arange(tm_e, dtype=I32)]).reshape(n_tiles + 1, 1, tm_e)

    y4 = _experts(tile_expert, n_used.astype(I32), dst, h2, p["w1"], p["b1"].reshape(N_EXPERTS, 1, -1),
                  p["w2"], p["b2"].reshape(N_EXPERTS, 1, -1), tm=tm_e)
    return x2, meta, y4


def _tiles(seq, n):
    return dict(tt_a=min(seq, 512), tt_b=min(seq, 512), tm_merge=min(n, 512), tm_router=min(n, 512),
                tm_expert=min(n, 512), tm_combine=min(n, 256))


def kernel(x, norm1_g, w_in, a_conv_w, a_A_log, a_dt_bias, a_norm_g, b_mu, b_w0, b_w2, b_a0, b_a2, b_g2, b_k_k, b_k_a, b_r_k, b_ln_w, b_ln_b, w_up_a, w_up_b, w_out, norm2_g, w_router, b_router, w1, b1, w2, b2, final_g):
    batch, seq, d = x.shape
    n = batch * seq
    depth = norm1_g.shape[0]
    tiles = _tiles(seq, n)
    params = dict(norm1_g=norm1_g, w_in=w_in, a_conv_w=a_conv_w, a_A_log=a_A_log, a_dt_bias=a_dt_bias,
                  a_norm_g=a_norm_g, b_mu=b_mu, b_w0=b_w0, b_w2=b_w2, b_a0=b_a0, b_a2=b_a2, b_g2=b_g2,
                  b_k_k=b_k_k, b_k_a=b_k_a, b_r_k=b_r_k, b_ln_w=b_ln_w, b_ln_b=b_ln_b, w_up_a=w_up_a,
                  w_up_b=w_up_b, w_out=w_out, norm2_g=norm2_g, w_router=w_router, b_router=b_router,
                  w1=w1, b1=b1, w2=w2, b2=b2)
    x2d = x.reshape(n, d)
    for l in range(depth):
        p = {k: v[l] for k, v in params.items()}
        x2, meta, y4 = _layer(x2d, p, batch=batch, seq=seq, tiles=tiles)
        x2d = _combine(x2, meta, final_g.reshape(1, d), y4, tm=tiles["tm_combine"],
                       final_norm=(l == depth - 1))
    return x2d.reshape(batch, seq, d)
```
